```python
import math
import jax, jax.numpy as jnp
from jax import lax
import numpy as np

D_MODEL = 4096
BATCH = 4
SEQ = 4096
DEPTH = 1

PLE_DIM = 256
ATTN_HEAD_DIM = 128
ATTN_HEADS_PER_GROUP = 4
DILATED_GROUPS = ((128, 1), (512, 4), (2048, 16))
N_ATTN_GROUPS = len(DILATED_GROUPS)
N_ATTN_HEADS = N_ATTN_GROUPS * ATTN_HEADS_PER_GROUP
ATTN_WIDTH = N_ATTN_HEADS * ATTN_HEAD_DIM
ATTN_OUT_WIDTH = ATTN_HEADS_PER_GROUP * ATTN_HEAD_DIM
ATTN_BLOCK = 128
N_BUCKETS = 32
MAX_DISTANCE = 2048
RWKV_HEAD_DIM = 64
RWKV_WIDTH = D_MODEL // 2
RWKV_HEADS = RWKV_WIDTH // RWKV_HEAD_DIM
DECAY_LORA = 96
AAA_LORA = 96
GATE_LORA = 256
RWKV_IN_WIDTH = 3 * RWKV_WIDTH + DECAY_LORA + AAA_LORA + GATE_LORA
D_FF = 4 * D_MODEL
N_BRANCHES = 2
IN_WIDTH = 3 * ATTN_WIDTH + RWKV_IN_WIDTH + N_BRANCHES * D_MODEL
RMS_EPS = 1e-6
GN_EPS = 64e-5

kernel_name = "hybrid_dilated_attn_rwkv7_gated_block"


def rms_norm(x, gain):
    x32 = x.astype(jnp.float32)
    y = x32 * lax.rsqrt(jnp.mean(jnp.square(x32), axis=-1, keepdims=True) + RMS_EPS)
    return (y * gain.astype(jnp.float32)).astype(x.dtype)


def t5_bucket(dist):
    max_exact = N_BUCKETS // 2
    d_f = jnp.maximum(dist, 1).astype(jnp.float32)
    large = max_exact + (jnp.log(d_f / max_exact) / math.log(MAX_DISTANCE / max_exact)
                         * (N_BUCKETS - max_exact)).astype(jnp.int32)
    large = jnp.minimum(large, N_BUCKETS - 1)
    return jnp.where(dist < max_exact, dist, large)


def dilated_window_attention(q, k, v, bias_table, window, dilation):
    batch, seq, heads, hd = q.shape
    n_dist = window // dilation
    blk = ATTN_BLOCK
    span = dilation * blk
    s_pad = -(-seq // span) * span
    length = s_pad // dilation
    nb = length // blk

    def to_blocks(t):
        t = jnp.pad(t, ((0, 0), (0, s_pad - seq), (0, 0), (0, 0)))
        t = t.reshape(batch, length, dilation, heads, hd).transpose(0, 2, 3, 1, 4)
        return t.reshape(batch, dilation, heads, nb, blk, hd)

    def with_prev(t):
        prev = jnp.pad(t[:, :, :, :-1], ((0, 0), (0, 0), (0, 0), (1, 0), (0, 0), (0, 0)))
        return jnp.concatenate([prev, t], axis=4)

    qb = to_blocks(q).astype(jnp.float32)
    kb = with_prev(to_blocks(k)).astype(jnp.float32)
    vb = with_prev(to_blocks(v)).astype(jnp.float32)
    s = jnp.einsum('bdhnqe,bdhnke->bdhnqk', qb, kb) * (hd ** -0.5)

    q_idx = blk + jnp.arange(blk)
    k_idx = jnp.arange(2 * blk)
    rel = q_idx[:, None] - k_idx[None, :]
    band = (rel >= 0) & (rel <= n_dist)
    first = (jnp.arange(nb)[:, None, None] == 0) & (k_idx[None, None, :] < blk)
    valid = band[None] & ~first
    bucket = t5_bucket(jnp.maximum(rel, 0) * dilation)
    bias = jnp.transpose(bias_table[bucket].astype(jnp.float32), (2, 0, 1))
    s = jnp.where(valid, s + bias[:, None], -jnp.inf)

    m = jnp.max(s, axis=-1, keepdims=True)
    e = jnp.exp(s - m)
    l = jnp.sum(e, axis=-1, keepdims=True)
    o = jnp.einsum('bdhnqk,bdhnke->bdhnqe', e, vb) / l
    lse = (m + jnp.log(l))[..., 0]

    o = o.reshape(batch, dilation, heads, length, hd).transpose(0, 3, 1, 2, 4)
    o = o.reshape(batch, s_pad, heads, hd)[:, :seq]
    lse = lse.reshape(batch, dilation, heads, length).transpose(0, 3, 1, 2)
    lse = lse.reshape(batch, s_pad, heads)[:, :seq]
    return o, lse


def token_shift(z, mix):
    prev = jnp.pad(z, ((0, 0), (1, 0), (0, 0)))[:, :-1]
    return z + mix * (prev - z)


def rwkv7_time_mix(z, w0, w_decay_up, a0, w_aaa_up, w_gate_up, k_k, k_a, r_k, gn_w, gn_b):
    batch, seq, _ = z.shape
    f32 = jnp.float32
    z = z.astype(f32)
    c0 = RWKV_WIDTH
    r = z[..., :c0]
    k = z[..., c0:2 * c0]
    v = z[..., 2 * c0:3 * c0]
    xw = z[..., 3 * c0:3 * c0 + DECAY_LORA]
    xa = z[..., 3 * c0 + DECAY_LORA:3 * c0 + DECAY_LORA + AAA_LORA]
    xg = z[..., 3 * c0 + DECAY_LORA + AAA_LORA:]

    w = -jax.nn.softplus(-(w0.astype(f32) + jnp.tanh(xw) @ w_decay_up.astype(f32))) - 0.5
    a = jax.nn.sigmoid(a0.astype(f32) + xa @ w_aaa_up.astype(f32))
    g = jax.nn.sigmoid(xg) @ w_gate_up.astype(f32)
    decay = jnp.exp(-jnp.exp(w))

    hs = (batch, seq, RWKV_HEADS, RWKV_HEAD_DIM)
    kk = (k * k_k.astype(f32)).reshape(hs)
    kk = kk / jnp.maximum(jnp.linalg.norm(kk, axis=-1, keepdims=True), 1e-12)
    k = k * (1.0 + (a - 1.0) * k_a.astype(f32))
    r_h, k_h, v_h, a_h, d_h = (t.reshape(hs) for t in (r, k, v, a, decay))

    def step(state, inp):
        r_t, w_t, k_t, v_t, aa_t, bb_t = inp
        sa = jnp.einsum('bhij,bhj->bhi', state, aa_t)
        state = (state * w_t[:, :, None, :] + sa[..., None] * bb_t[:, :, None, :]
                 + v_t[..., None] * k_t[:, :, None, :])
        return state, jnp.einsum('bhij,bhj->bhi', state, r_t)

    xs = tuple(jnp.moveaxis(t, 1, 0) for t in (r_h, d_h, k_h, v_h, -kk, kk * a_h))
    state0 = jnp.zeros((batch, RWKV_HEADS, RWKV_HEAD_DIM, RWKV_HEAD_DIM), f32)
    _, y = lax.scan(step, state0, xs)
    y = jnp.moveaxis(y, 0, 1)

    mu = jnp.mean(y, axis=-1, keepdims=True)
    var = jnp.mean(jnp.square(y - mu), axis=-1, keepdims=True)
    y = ((y - mu) * lax.rsqrt(var + GN_EPS)).reshape(batch, seq, RWKV_WIDTH)
    y = (y * gn_w.astype(f32) + gn_b.astype(f32)).reshape(hs)
    y = y + jnp.sum(r_h * k_h * r_k.astype(f32), axis=-1, keepdims=True) * v_h
    return y.reshape(batch, seq, RWKV_WIDTH) * g


def setup_inputs(seed: int = 0) -> dict:
    key = jax.random.key(seed)
    ks = jax.random.split(key, 32)
    f32 = jnp.float32

    def nrm(k, shape, scale):
        return jax.random.normal(k, shape, f32) * scale

    def gain(k, shape):
        return 1.0 + 0.05 * jax.random.normal(k, shape, f32)

    L = DEPTH
    return {
        "x": nrm(ks[0], (BATCH, SEQ, D_MODEL), 1.0),
        "p": nrm(ks[1], (DEPTH, BATCH, SEQ, PLE_DIM), 1.0),
        "norm_mix": gain(ks[2], (L, D_MODEL)),
        "w_in": nrm(ks[3], (L, D_MODEL, IN_WIDTH), D_MODEL ** -0.5),
        "q_gain": gain(ks[4], (L, ATTN_HEAD_DIM)),
        "k_gain": gain(ks[5], (L, ATTN_HEAD_DIM)),
        "rel_bias": nrm(ks[6], (N_BUCKETS, N_ATTN_HEADS), 0.5),
        "w_attn_up": nrm(ks[7], (L, ATTN_OUT_WIDTH, D_MODEL), ATTN_OUT_WIDTH ** -0.5),
        "shift_mix": jax.random.uniform(ks[8], (L, RWKV_IN_WIDTH), f32),
        "w0": jax.random.uniform(ks[9], (L, RWKV_WIDTH), f32, minval=-5.0, maxval=1.0),
        "w_decay_up": nrm(ks[10], (L, DECAY_LORA, RWKV_WIDTH), DECAY_LORA ** -0.5),
        "a0": nrm(ks[11], (L, RWKV_WIDTH), 0.1),
        "w_aaa_up": nrm(ks[12], (L, AAA_LORA, RWKV_WIDTH), AAA_LORA ** -0.5),
        "w_gate_up": nrm(ks[13], (L, GATE_LORA, RWKV_WIDTH), GATE_LORA ** -0.5),
        "k_k": 0.85 + nrm(ks[14], (L, RWKV_WIDTH), 0.05),
        "k_a": gain(ks[15], (L, RWKV_WIDTH)),
        "r_k": nrm(ks[16], (L, RWKV_HEADS, RWKV_HEAD_DIM), 0.1),
        "gn_w": gain(ks[17], (L, RWKV_WIDTH)),
        "gn_b": nrm(ks[18], (L, RWKV_WIDTH), 0.02),
        "w_rwkv_up": nrm(ks[19], (L, RWKV_WIDTH, D_MODEL), RWKV_WIDTH ** -0.5),
        "w_out": nrm(ks[20], (L, D_MODEL, D_MODEL), D_MODEL ** -0.5),
        "norm_mlp": gain(ks[21], (L, D_MODEL)),
        "w_mlp_in": nrm(ks[22], (L, D_MODEL, D_FF), D_MODEL ** -0.5),
        "w_mlp_out": nrm(ks[23], (L, D_FF, D_MODEL), D_FF ** -0.5),
        "norm_ple": gain(ks[24], (L, D_MODEL)),
        "w_ple_gate": nrm(ks[25], (L, D_MODEL, D_MODEL), D_MODEL ** -0.5),
        "w_ple_proj": nrm(ks[26], (L, PLE_DIM, D_MODEL), PLE_DIM ** -0.5),
    }


def reference(x, p, norm_mix, w_in, q_gain, k_gain, rel_bias, w_attn_up, shift_mix, w0, w_decay_up,
              a0, w_aaa_up, w_gate_up, k_k, k_a, r_k, gn_w, gn_b, w_rwkv_up, w_out, norm_mlp,
              w_mlp_in, w_mlp_out, norm_ple, w_ple_gate, w_ple_proj):
    batch, seq, _ = x.shape
    a_end = 3 * ATTN_WIDTH
    r_end = a_end + RWKV_IN_WIDTH
    for i in range(DEPTH):
        h = rms_norm(x, norm_mix[i])
        proj = h @ w_in[i]
        qkv = proj[..., :a_end].reshape(batch, seq, 3, N_ATTN_HEADS, ATTN_HEAD_DIM)
        z = proj[..., a_end:r_end]
        gates = jax.nn.sigmoid(proj[..., r_end:]).reshape(batch, seq, N_BRANCHES, D_MODEL)

        q = rms_norm(qkv[:, :, 0], q_gain[i])
        k = rms_norm(qkv[:, :, 1], k_gain[i])
        v = qkv[:, :, 2]
        outs, lses = [], []
        for gi, (window, dilation) in enumerate(DILATED_GROUPS):
            sl = slice(gi * ATTN_HEADS_PER_GROUP, (gi + 1) * ATTN_HEADS_PER_GROUP)
            o, lse = dilated_window_attention(q[:, :, sl], k[:, :, sl], v[:, :, sl],
                                              rel_bias[:, sl], window, dilation)
            outs.append(o)
            lses.append(lse)
        mix_w = jax.nn.softmax(jnp.stack(lses, axis=0), axis=0)
        attn = jnp.sum(mix_w[..., None] * jnp.stack(outs, axis=0), axis=0)
        attn = attn.reshape(batch, seq, ATTN_OUT_WIDTH).astype(x.dtype)
        attn_d = attn @ w_attn_up[i]

        zs = token_shift(z, shift_mix[i])
        rw = rwkv7_time_mix(zs, w0[i], w_decay_up[i], a0[i], w_aaa_up[i], w_gate_up[i],
                            k_k[i], k_a[i], r_k[i], gn_w[i], gn_b[i]).astype(x.dtype)
        rwkv_d = rw @ w_rwkv_up[i]

        merged = gates[:, :, 0] * attn_d + gates[:, :, 1] * rwkv_d
        x = x + merged @ w_out[i]

        h = rms_norm(x, norm_mlp[i])
        x = x + jnp.square(jax.nn.relu(h @ w_mlp_in[i])) @ w_mlp_out[i]

        ple_gate = jax.nn.sigmoid(rms_norm(x, norm_ple[i]) @ w_ple_gate[i])
        x = x + ple_gate * (p[i] @ w_ple_proj[i])
    return x
```

```python
import functools
import math

import jax
import jax.numpy as jnp
from jax import lax
from jax.experimental import pallas as pl
from jax.experimental.pallas import tpu as pltpu

F32 = jnp.float32
BF16 = jnp.bfloat16

LANES = 128
ATTN_HEAD_DIM = 128
ATTN_HEADS_PER_GROUP = 4
DILATED_GROUPS = ((128, 1), (512, 4), (2048, 16))
ATTN_BLOCK = 128
N_BUCKETS = 32
MAX_DISTANCE = 2048
RWKV_HEAD_DIM = 64
DECAY_LORA = 96
AAA_LORA = 96
GATE_LORA = 256
RMS_EPS = 1e-6
GN_EPS = 64e-5
MASK_VALUE = -1e30

RWKV_CHUNK = 64
RWKV_TBLK = 256
RWKV_HPB = 8

VMEM_LIMIT = 56 * 1024 * 1024


def _cparams(sem):
    return pltpu.CompilerParams(dimension_semantics=sem, vmem_limit_bytes=VMEM_LIMIT)


def _dot(a, b):
    return jnp.dot(a, b, preferred_element_type=F32)


def _dot_nt(a, b):
    return lax.dot_general(a, b, (((1,), (1,)), ((), ())), preferred_element_type=F32)


def _dot_split2(x, w_bf16):
    hi = x.astype(BF16)
    lo = (x - hi.astype(F32)).astype(BF16)
    return _dot(hi, w_bf16) + _dot(lo, w_bf16)


def _dot_split3_left(w_bf16, x):
    hi = x.astype(BF16)
    r1 = x - hi.astype(F32)
    mid = r1.astype(BF16)
    lo = (r1 - mid.astype(F32)).astype(BF16)
    return _dot(w_bf16, hi) + _dot(w_bf16, mid) + _dot(w_bf16, lo)


def _rmsnorm_kernel(x_ref, g_ref, o_ref):
    x = x_ref[...]
    ms = jnp.mean(x * x, axis=-1, keepdims=True)
    o_ref[...] = (x * lax.rsqrt(ms + RMS_EPS) * g_ref[...]).astype(o_ref.dtype)


def _rmsnorm_bf16(x, gain, tr=256):
    m, d = x.shape
    return pl.pallas_call(
        _rmsnorm_kernel,
        grid=(m // tr,),
        in_specs=[pl.BlockSpec((tr, d), lambda i: (i, 0)), pl.BlockSpec((1, d), lambda i: (0, 0))],
        out_specs=pl.BlockSpec((tr, d), lambda i: (i, 0)),
        out_shape=jax.ShapeDtypeStruct((m, d), BF16),
        compiler_params=_cparams(("parallel",)),
        name="rmsnorm",
    )(x, gain.reshape(1, d))


def _qkv_kernel(a_ref, w_ref, g_ref, o_ref, *, n_norm_tiles, heads_per_tile):
    j = pl.program_id(1)
    acc = _dot(a_ref[...], w_ref[...])

    @pl.when(j < n_norm_tiles)
    def _():
        for h in range(heads_per_tile):
            y = acc[:, h * LANES:(h + 1) * LANES]
            ms = jnp.mean(y * y, axis=-1, keepdims=True)
            o_ref[h] = (y * lax.rsqrt(ms + RMS_EPS) * g_ref[...]).astype(o_ref.dtype)

    @pl.when(j >= n_norm_tiles)
    def _():
        for h in range(heads_per_tile):
            o_ref[h] = acc[:, h * LANES:(h + 1) * LANES].astype(o_ref.dtype)


def _qkv_proj(h, w, gains, batch, seq, tm=1024, tn=512):
    m, d = h.shape
    n = w.shape[1]
    hpt = tn // LANES
    n_tiles = n // tn
    tiles_per_sec = n_tiles // 3
    mt_per_b = seq // tm
    kern = functools.partial(_qkv_kernel, n_norm_tiles=2 * tiles_per_sec, heads_per_tile=hpt)
    return pl.pallas_call(
        kern,
        grid=(m // tm, n_tiles),
        in_specs=[
            pl.BlockSpec((tm, d), lambda i, j: (i, 0)),
            pl.BlockSpec((d, tn), lambda i, j: (0, j)),
            pl.BlockSpec((None, 1, LANES), lambda i, j: (j // tiles_per_sec, 0, 0)),
        ],
        out_specs=pl.BlockSpec((None, hpt, tm, LANES), lambda i, j: (i // mt_per_b, j, i % mt_per_b, 0)),
        out_shape=jax.ShapeDtypeStruct((batch, n // LANES, seq, LANES), BF16),
        compiler_params=_cparams(("parallel", "arbitrary")),
        name="w_in_qkv",
    )(h, w, gains)


def _mm_kernel(a_ref, w_ref, o_ref, *, act):
    acc = _dot(a_ref[...], w_ref[...])
    if act == "sigmoid":
        acc = jax.nn.sigmoid(acc)
    elif act == "relu2":
        acc = jnp.square(jnp.maximum(acc, 0.0))
    o_ref[...] = acc.astype(o_ref.dtype)


def _matmul(a, w, *, act=None, out_dtype=F32, tm=1024, tn=512, name="matmul"):
    m, k = a.shape
    n = w.shape[1]
    tn = min(tn, n)
    return pl.pallas_call(
        functools.partial(_mm_kernel, act=act),
        grid=(m // tm, n // tn),
        in_specs=[pl.BlockSpec((tm, k), lambda i, j: (i, 0)), pl.BlockSpec((k, tn), lambda i, j: (0, j))],
        out_specs=pl.BlockSpec((tm, tn), lambda i, j: (i, j)),
        out_shape=jax.ShapeDtypeStruct((m, n), out_dtype),
        compiler_params=_cparams(("parallel", "arbitrary")),
        name=name,
    )(a, w)


def _mm_res_kernel(a_ref, w_ref, r_ref, o_ref):
    o_ref[...] = r_ref[...] + _dot(a_ref[...], w_ref[...])


def _matmul_residual(a, w, res, *, tm=1024, tn=512, name="matmul_res"):
    m, k = a.shape
    n = w.shape[1]
    return pl.pallas_call(
        _mm_res_kernel,
        grid=(m // tm, n // tn),
        in_specs=[
            pl.BlockSpec((tm, k), lambda i, j: (i, 0)),
            pl.BlockSpec((k, tn), lambda i, j: (0, j)),
            pl.BlockSpec((tm, tn), lambda i, j: (i, j)),
        ],
        out_specs=pl.BlockSpec((tm, tn), lambda i, j: (i, j)),
        out_shape=jax.ShapeDtypeStruct((m, n), F32),
        compiler_params=_cparams(("parallel", "arbitrary")),
        name=name,
    )(a, w, res)


def _mm_kacc_res_kernel(a_ref, w_ref, r_ref, o_ref, acc_ref):
    k = pl.program_id(2)

    @pl.when(k == 0)
    def _():
        acc_ref[...] = r_ref[...]

    acc_ref[...] += _dot(a_ref[...], w_ref[...])

    @pl.when(k == pl.num_programs(2) - 1)
    def _():
        o_ref[...] = acc_ref[...]


def _matmul_kacc_residual(a, w, res, *, tm=1024, tn=1024, tk=2048, name="matmul_kacc"):
    m, kdim = a.shape
    n = w.shape[1]
    return pl.pallas_call(
        _mm_kacc_res_kernel,
        grid=(m // tm, n // tn, kdim // tk),
        in_specs=[
            pl.BlockSpec((tm, tk), lambda i, j, k: (i, k)),
            pl.BlockSpec((tk, tn), lambda i, j, k: (k, j)),
            pl.BlockSpec((tm, tn), lambda i, j, k: (i, j)),
        ],
        out_specs=pl.BlockSpec((tm, tn), lambda i, j, k: (i, j)),
        out_shape=jax.ShapeDtypeStruct((m, n), F32),
        scratch_shapes=[pltpu.VMEM((tm, tn), F32)],
        compiler_params=_cparams(("parallel", "parallel", "arbitrary")),
        name=name,
    )(a, w, res)


def _t5_bucket(dist):
    max_exact = N_BUCKETS // 2
    d_f = jnp.maximum(dist, 1).astype(F32)
    large = max_exact + (jnp.log(d_f / max_exact) / math.log(MAX_DISTANCE / max_exact)
                         * (N_BUCKETS - max_exact)).astype(jnp.int32)
    large = jnp.minimum(large, N_BUCKETS - 1)
    return jnp.where(dist < max_exact, dist, large)


def _band_bias(rel_bias_group, window, dilation):
    blk = ATTN_BLOCK
    n_dist = window // dilation
    rel = (blk + jnp.arange(blk))[:, None] - jnp.arange(2 * blk)[None, :]
    band = (rel >= 0) & (rel <= n_dist)
    bucket = _t5_bucket(jnp.maximum(rel, 0) * dilation)
    bias = jnp.transpose(rel_bias_group.astype(F32)[bucket], (2, 0, 1))
    return jnp.where(band[None], bias, MASK_VALUE)


def _attn_kernel(q_ref, k_ref, v_ref, bias_ref, o_ref, lse_ref, *, dilation, n_blocks):
    blk = ATTN_BLOCK
    bias = bias_ref[...]

    def softmax_out(s, v):
        m = jnp.max(s, axis=-1, keepdims=True)
        e = jnp.exp(s - m)
        l = jnp.sum(e, axis=-1, keepdims=True)
        o = _dot(e.astype(BF16), v) / l
        return o, jnp.broadcast_to(m + jnp.log(l), (blk, LANES))

    for r in range(dilation):
        cs = slice(r * LANES, (r + 1) * LANES)
        s0 = _dot_nt(q_ref[0:blk, cs], k_ref[0:blk, cs]) + bias[:, blk:]
        o0, lse0 = softmax_out(s0, v_ref[0:blk, cs])
        o_ref[0:blk, cs] = o0
        lse_ref[0:blk, cs] = lse0

        def body(n, carry, cs=cs):
            off = pl.multiple_of(n * blk, blk)
            s = _dot_nt(q_ref[pl.ds(off, blk), cs], k_ref[pl.ds(off - blk, 2 * blk), cs]) + bias
            o, lse = softmax_out(s, v_ref[pl.ds(off - blk, 2 * blk), cs])
            o_ref[pl.ds(off, blk), cs] = o
            lse_ref[pl.ds(off, blk), cs] = lse
            return carry

        lax.fori_loop(1, n_blocks, body, 0)


def _dilated_attention(qkvh, biasm, group, dilation, batch, seq):
    n_heads = qkvh.shape[1] // 3
    hg = ATTN_HEADS_PER_GROUP
    length = seq // dilation
    width = dilation * LANES
    view = qkvh.reshape(batch, 3 * n_heads, length, width)
    n_blocks = length // ATTN_BLOCK
    kern = functools.partial(_attn_kernel, dilation=dilation, n_blocks=n_blocks)

    def head_spec(section):
        return pl.BlockSpec((None, None, length, width),
                            lambda b, h: (b, section * n_heads + group * hg + h, 0, 0))

    out_spec = pl.BlockSpec((None, None, length, width), lambda b, h: (b, h, 0, 0))
    o, lse = pl.pallas_call(
        kern,
        grid=(batch, hg),
        in_specs=[head_spec(0), head_spec(1), head_spec(2),
                  pl.BlockSpec((None, ATTN_BLOCK, 2 * ATTN_BLOCK), lambda b, h: (h, 0, 0))],
        out_specs=[out_spec, out_spec],
        out_shape=[jax.ShapeDtypeStruct((batch, hg, length, width), F32)] * 2,
        compiler_params=_cparams(("parallel", "parallel")),
        name=f"dilated_attn_d{dilation}",
    )(view, view, view, biasm)
    return o.reshape(batch, hg, seq, LANES), lse.reshape(batch, hg, seq, LANES)


def _rwkv_kernel(r_ref, k_ref, v_ref, zl_ref, mr_ref, mk_ref, mv_ref, ml_ref, wl_ref,
                 w0_ref, a0_ref, kk_ref, ka_ref, rk_ref, gw_ref, gb_ref, seg_ref, tri_ref,
                 o_ref,
                 state_ref, prev_ref, prevl_ref, at_ref, rt_ref, bt_ref, kt_ref, bh_ref, kh_ref,
                 vb_ref, egc_ref, y_ref, *, lora_dims):
    tblk, cw = r_ref.shape
    ch = RWKV_CHUNK
    n_chunks = tblk // ch
    n_pairs = cw // LANES
    hd = RWKV_HEAD_DIM
    t = pl.program_id(2)

    @pl.when(t == 0)
    def _():
        state_ref[...] = jnp.zeros_like(state_ref)
        prev_ref[...] = jnp.zeros_like(prev_ref)
        prevl_ref[...] = jnp.zeros_like(prevl_ref)

    row = lax.broadcasted_iota(jnp.int32, (tblk, 1), 0)

    def shift(z, prev_row, mix):
        prev = jnp.where(row == 0, prev_row, pltpu.roll(z, 1, axis=0))
        return z + mix * (prev - z)

    zr = r_ref[...]
    zk = k_ref[...]
    zv = v_ref[...]
    zl = zl_ref[...]
    r = shift(zr, prev_ref[0:1, :], mr_ref[...])
    k = shift(zk, prev_ref[1:2, :], mk_ref[...])
    v = shift(zv, prev_ref[2:3, :], mv_ref[...])
    xl = shift(zl, prevl_ref[0:1, :], ml_ref[...])
    prev_ref[0:1, :] = zr[tblk - 1:tblk, :]
    prev_ref[1:2, :] = zk[tblk - 1:tblk, :]
    prev_ref[2:3, :] = zv[tblk - 1:tblk, :]
    prevl_ref[0:1, :] = zl[tblk - 1:tblk, :]

    d_w, d_a, _ = lora_dims
    lane_l = lax.broadcasted_iota(jnp.int32, xl.shape, 1)
    act = jnp.where(lane_l < d_w, jnp.tanh(xl), jnp.where(lane_l < d_w + d_a, xl, jax.nn.sigmoid(xl)))
    up = _dot(act.astype(BF16), wl_ref[...])
    w_arg = -(w0_ref[...] + up[:, 0:cw])
    softplus = jnp.maximum(w_arg, 0.0) + jnp.log1p(jnp.exp(-jnp.abs(w_arg)))
    logdecay = -jnp.exp(-softplus - 0.5)
    a_gate = jax.nn.sigmoid(a0_ref[...] + up[:, cw:2 * cw])
    g_gate = up[:, 2 * cw:3 * cw]

    seg = seg_ref[...]
    kk = k * kk_ref[...]
    kk_norm = jnp.sqrt(_dot_split2(kk * kk, seg))
    kk = kk / jnp.maximum(kk_norm, 1e-12)
    k2 = k * (1.0 + (a_gate - 1.0) * ka_ref[...])
    a_vec = -kk
    b_vec = kk * a_gate
    bonus = _dot_split2(r * k2 * rk_ref[...], seg) * v

    tri = tri_ref[...]
    for c in range(n_chunks):
        rows = slice(c * ch, (c + 1) * ch)
        ld = logdecay[rows]
        g_inc = _dot_split3_left(tri, ld)
        g_exc = g_inc - ld
        g_last = g_inc[ch - 1:ch, :]
        e_neg = jnp.exp(-g_inc)
        e_tail = jnp.exp(g_last - g_inc)
        at_ref[rows, :] = (a_vec[rows] * jnp.exp(g_exc)).astype(BF16)
        rt_ref[rows, :] = (r[rows] * jnp.exp(g_inc)).astype(BF16)
        bt_ref[rows, :] = (b_vec[rows] * e_neg).astype(BF16)
        kt_ref[rows, :] = (k2[rows] * e_neg).astype(BF16)
        bh_ref[rows, :] = (b_vec[rows] * e_tail).astype(BF16)
        kh_ref[rows, :] = (k2[rows] * e_tail).astype(BF16)
        egc_ref[c] = jnp.broadcast_to(jnp.exp(g_last), egc_ref.shape[1:])
    vb_ref[...] = v.astype(BF16)

    lane = lax.broadcasted_iota(jnp.int32, (1, LANES), 1)
    head_masks = (lane < hd, lane >= hd)
    rr = lax.broadcasted_iota(jnp.int32, (ch, 2 * ch), 0)
    cc = lax.broadcasted_iota(jnp.int32, (ch, 2 * ch), 1)
    cc = jnp.where(cc >= ch, cc - ch, cc)
    strict_lower = cc < rr
    incl_lower = cc <= rr
    eye = (lax.broadcasted_iota(jnp.int32, (ch, ch), 0)
           == lax.broadcasted_iota(jnp.int32, (ch, ch), 1)).astype(F32)
    bd_r = lax.broadcasted_iota(jnp.int32, (LANES, LANES), 0) >= hd
    bd_c = lax.broadcasted_iota(jnp.int32, (LANES, LANES), 1) >= hd
    block_diag = bd_r == bd_c
    zeros_cv = jnp.zeros((ch, LANES), BF16)

    def chunk_body(c, carry):
        off = pl.multiple_of(c * ch, ch)
        rows = pl.ds(off, ch)
        for p in range(n_pairs):
            cs = slice(p * LANES, (p + 1) * LANES)
            at = at_ref[rows, cs]
            rt = rt_ref[rows, cs]
            vb = vb_ref[rows, cs]
            x = jnp.concatenate([at, rt], axis=0)
            y_bk = jnp.concatenate([bt_ref[rows, cs], kt_ref[rows, cs]], axis=0)
            zero_v = jnp.concatenate([zeros_cv, vb], axis=0)
            ht = state_ref[p]
            xh = _dot_nt(x, ht.astype(BF16))
            u_pair = None
            y_pair = None
            for h in range(2):
                mh = head_masks[h]
                g_all = _dot_nt(x, jnp.where(mh, y_bk, jnp.zeros_like(y_bk)))
                g1 = jnp.where(strict_lower, g_all[:ch], 0.0)
                g2 = jnp.where(incl_lower, g_all[ch:], 0.0)
                pw = g1[:, :ch]
                tinv = eye + pw
                for _ in range(int(math.log2(ch)) - 1):
                    pb = pw.astype(BF16)
                    pw = _dot(pb, pb)
                    tinv = tinv + _dot(tinv.astype(BF16), pw.astype(BF16))
                z = xh[:ch] + _dot(g1.astype(BF16), zero_v)
                u = _dot(tinv.astype(BF16), z.astype(BF16))
                uv = jnp.concatenate([u.astype(BF16), vb], axis=0)
                yo = xh[ch:] + _dot(g2.astype(BF16), uv)
                u_pair = u if h == 0 else jnp.where(mh, u, u_pair)
                y_pair = yo if h == 0 else jnp.where(mh, yo, y_pair)
            y_ref[rows, cs] = y_pair
            uv_t = jnp.concatenate([u_pair, vb.astype(F32)], axis=0).T.astype(BF16)
            bk_hat = jnp.concatenate([bh_ref[rows, cs], kh_ref[rows, cs]], axis=0)
            upd = _dot(uv_t, bk_hat)
            state_ref[p] = ht * egc_ref[c, 0:1, cs] + jnp.where(block_diag, upd, 0.0)
        return carry

    lax.fori_loop(0, n_chunks, chunk_body, 0)

    y = y_ref[...]
    inv_hd = 1.0 / hd
    mu = _dot_split2(y, seg) * inv_hd
    yc = y - mu
    var = _dot_split2(yc * yc, seg) * inv_hd
    yn = yc * lax.rsqrt(var + GN_EPS) * gw_ref[...] + gb_ref[...]
    o_ref[...] = ((yn + bonus) * g_gate).astype(o_ref.dtype)


def _rwkv7(zr, zl, shift_mix, w0, w_decay_up, a0, w_aaa_up, w_gate_up, k_k, k_a, r_k, gn_w, gn_b,
           batch, seq):
    tok = zr.shape[0]
    width = zr.shape[1] // 3
    lw = zl.shape[1]
    cw = RWKV_HPB * RWKV_HEAD_DIM
    nhb = width // cw
    tblk = RWKV_TBLK
    nt = seq // tblk
    ch = RWKV_CHUNK

    def pad_rows(w, start):
        return jnp.pad(w, ((start, lw - start - w.shape[0]), (0, 0)))

    wl = jnp.stack([pad_rows(w_decay_up, 0), pad_rows(w_aaa_up, DECAY_LORA),
                    pad_rows(w_gate_up, DECAY_LORA + AAA_LORA)], axis=0)
    wl = wl.reshape(3, lw, nhb, cw).transpose(2, 1, 0, 3).reshape(nhb, lw, 3 * cw).astype(BF16)

    mix_z = shift_mix[: 3 * width].reshape(1, 3 * width)
    mix_l = shift_mix[3 * width:].reshape(1, lw)
    idx = jnp.arange(cw) // RWKV_HEAD_DIM
    seg = (idx[:, None] == idx[None, :]).astype(BF16)
    tri = (jnp.arange(ch)[:, None] >= jnp.arange(ch)[None, :]).astype(BF16)

    def col(sec):
        return pl.BlockSpec((tblk, cw), lambda b, h, t: (b * nt + t, sec * nhb + h))

    def mixcol(sec):
        return pl.BlockSpec((1, cw), lambda b, h, t: (0, sec * nhb + h))

    vec = pl.BlockSpec((1, cw), lambda b, h, t: (0, h))
    row_vec = lambda a: a.reshape(1, width)
    kern = functools.partial(_rwkv_kernel, lora_dims=(DECAY_LORA, AAA_LORA, GATE_LORA))
    return pl.pallas_call(
        kern,
        grid=(batch, nhb, nt),
        in_specs=[
            col(0), col(1), col(2),
            pl.BlockSpec((tblk, lw), lambda b, h, t: (b * nt + t, 0)),
            mixcol(0), mixcol(1), mixcol(2),
            pl.BlockSpec((1, lw), lambda b, h, t: (0, 0)),
            pl.BlockSpec((None, lw, 3 * cw), lambda b, h, t: (h, 0, 0)),
            vec, vec, vec, vec, vec, vec, vec,
            pl.BlockSpec((cw, cw), lambda b, h, t: (0, 0)),
            pl.BlockSpec((ch, ch), lambda b, h, t: (0, 0)),
        ],
        out_specs=pl.BlockSpec((tblk, cw), lambda b, h, t: (b * nt + t, h)),
        out_shape=jax.ShapeDtypeStruct((tok, width), BF16),
        scratch_shapes=[
            pltpu.VMEM((cw // LANES, LANES, LANES), F32),
            pltpu.VMEM((8, cw), F32),
            pltpu.VMEM((8, lw), F32),
            pltpu.VMEM((tblk, cw), BF16), pltpu.VMEM((tblk, cw), BF16), pltpu.VMEM((tblk, cw), BF16),
            pltpu.VMEM((tblk, cw), BF16), pltpu.VMEM((tblk, cw), BF16), pltpu.VMEM((tblk, cw), BF16),
            pltpu.VMEM((tblk, cw), BF16),
            pltpu.VMEM((tblk // ch, 8, cw), F32),
            pltpu.VMEM((tblk, cw), F32),
        ],
        compiler_params=_cparams(("parallel", "parallel", "arbitrary")),
        name="rwkv7_chunked",
    )(zr, zr, zr, zl, mix_z, mix_z, mix_z, mix_l, wl,
      row_vec(w0), row_vec(a0), row_vec(k_k), row_vec(k_a), row_vec(r_k), row_vec(gn_w), row_vec(gn_b),
      seg, tri)


def _merge_kernel(o0_ref, o1_ref, o2_ref, l0_ref, l1_ref, l2_ref, rw_ref, g0_ref, g1_ref, wa_ref, wr_ref,
                  out_ref, attn_ref):
    j = pl.program_id(1)

    @pl.when(j == 0)
    def _():
        for h in range(ATTN_HEADS_PER_GROUP):
            l0, l1, l2 = l0_ref[h], l1_ref[h], l2_ref[h]
            m = jnp.maximum(jnp.maximum(l0, l1), l2)
            e0, e1, e2 = jnp.exp(l0 - m), jnp.exp(l1 - m), jnp.exp(l2 - m)
            num = e0 * o0_ref[h] + e1 * o1_ref[h] + e2 * o2_ref[h]
            attn_ref[:, h * LANES:(h + 1) * LANES] = (num / (e0 + e1 + e2)).astype(BF16)

    attn_d = _dot(attn_ref[...], wa_ref[...])
    rwkv_d = _dot(rw_ref[...], wr_ref[...])
    out_ref[...] = (g0_ref[...].astype(F32) * attn_d + g1_ref[...].astype(F32) * rwkv_d).astype(out_ref.dtype)


def _merge(os, lses, rw, gates, w_attn_up, w_rwkv_up, batch, seq, tm=512, tn=512):
    tok = rw.shape[0]
    d = w_attn_up.shape[1]
    hg = ATTN_HEADS_PER_GROUP
    mt_per_b = seq // tm
    nj = d // tn
    head_spec = pl.BlockSpec((None, hg, tm, LANES), lambda i, j: (i // mt_per_b, 0, i % mt_per_b, 0))
    return pl.pallas_call(
        _merge_kernel,
        grid=(tok // tm, nj),
        in_specs=[head_spec] * 6 + [
            pl.BlockSpec((tm, rw.shape[1]), lambda i, j: (i, 0)),
            pl.BlockSpec((tm, tn), lambda i, j: (i, j)),
            pl.BlockSpec((tm, tn), lambda i, j: (i, nj + j)),
            pl.BlockSpec((w_attn_up.shape[0], tn), lambda i, j: (0, j)),
            pl.BlockSpec((w_rwkv_up.shape[0], tn), lambda i, j: (0, j)),
        ],
        out_specs=pl.BlockSpec((tm, tn), lambda i, j: (i, j)),
        out_shape=jax.ShapeDtypeStruct((tok, d), BF16),
        scratch_shapes=[pltpu.VMEM((tm, hg * LANES), BF16)],
        compiler_params=_cparams(("parallel", "arbitrary")),
        name="merge_gated_up",
    )(*os, *lses, rw, gates, gates, w_attn_up, w_rwkv_up)


def _ple_kernel(h_ref, wg_ref, p_ref, wp_ref, x_ref, o_ref):
    gate = jax.nn.sigmoid(_dot(h_ref[...], wg_ref[...]))
    o_ref[...] = x_ref[...] + gate * _dot(p_ref[...], wp_ref[...])


def _ple(hn, w_gate, p, w_proj, x, tm=1024, tn=512):
    m, k = hn.shape
    n = w_gate.shape[1]
    kp = p.shape[1]
    return pl.pallas_call(
        _ple_kernel,
        grid=(m // tm, n // tn),
        in_specs=[
            pl.BlockSpec((tm, k), lambda i, j: (i, 0)),
            pl.BlockSpec((k, tn), lambda i, j: (0, j)),
            pl.BlockSpec((tm, kp), lambda i, j: (i, 0)),
            pl.BlockSpec((kp, tn), lambda i, j: (0, j)),
            pl.BlockSpec((tm, tn), lambda i, j: (i, j)),
        ],
        out_specs=pl.BlockSpec((tm, tn), lambda i, j: (i, j)),
        out_shape=jax.ShapeDtypeStruct((m, n), F32),
        compiler_params=_cparams(("parallel", "arbitrary")),
        name="ple_gated",
    )(hn, w_gate, p, w_proj, x)


def _layer(x2d, p2d, batch, seq, norm_mix, w_in, q_gain, k_gain, rel_bias, w_attn_up, shift_mix, w0,
           w_decay_up, a0, w_aaa_up, w_gate_up, k_k, k_a, r_k, gn_w, gn_b, w_rwkv_up, w_out, norm_mlp,
           w_mlp_in, w_mlp_out, norm_ple, w_ple_gate, w_ple_proj):
    d_model = x2d.shape[1]
    n_attn_heads = len(DILATED_GROUPS) * ATTN_HEADS_PER_GROUP
    attn_width = n_attn_heads * ATTN_HEAD_DIM
    rwkv_width = w_rwkv_up.shape[0]
    a_end = 3 * attn_width
    z_end = a_end + 3 * rwkv_width
    r_end = z_end + DECAY_LORA + AAA_LORA + GATE_LORA

    h = _rmsnorm_bf16(x2d, norm_mix)

    gains = jnp.stack([q_gain * (ATTN_HEAD_DIM ** -0.5), k_gain, jnp.ones_like(q_gain)]).reshape(3, 1, LANES)
    qkvh = _qkv_proj(h, w_in[:, :a_end].astype(BF16), gains, batch, seq)
    zr = _matmul(h, w_in[:, a_end:z_end].astype(BF16), name="w_in_rkv")
    zl = _matmul(h, w_in[:, z_end:r_end].astype(BF16), name="w_in_lora")
    gates = _matmul(h, w_in[:, r_end:].astype(BF16), act="sigmoid", out_dtype=BF16, name="w_in_gates")

    os, lses = [], []
    for gi, (window, dilation) in enumerate(DILATED_GROUPS):
        sl = slice(gi * ATTN_HEADS_PER_GROUP, (gi + 1) * ATTN_HEADS_PER_GROUP)
        biasm = _band_bias(rel_bias[:, sl], window, dilation)
        o, lse = _dilated_attention(qkvh, biasm, gi, dilation, batch, seq)
        os.append(o)
        lses.append(lse)

    rw = _rwkv7(zr, zl, shift_mix, w0, w_decay_up, a0, w_aaa_up, w_gate_up, k_k, k_a, r_k, gn_w, gn_b,
                batch, seq)

    merged = _merge(os, lses, rw, gates, w_attn_up.astype(BF16), w_rwkv_up.astype(BF16), batch, seq)
    x1 = _matmul_residual(merged, w_out.astype(BF16), x2d, name="w_out_res")

    h2 = _rmsnorm_bf16(x1, norm_mlp)
    u = _matmul(h2, w_mlp_in.astype(BF16), act="relu2", out_dtype=BF16, name="mlp_in")
    x2 = _matmul_kacc_residual(u, w_mlp_out.astype(BF16), x1, name="mlp_out_res")

    hn = _rmsnorm_bf16(x2, norm_ple)
    del d_model
    return _ple(hn, w_ple_gate.astype(BF16), p2d.astype(BF16), w_ple_proj.astype(BF16), x2)


def kernel(x, p, norm_mix, w_in, q_gain, k_gain, rel_bias, w_attn_up, shift_mix, w0, w_decay_up, a0, w_aaa_up,
           w_gate_up, k_k, k_a, r_k, gn_w, gn_b, w_rwkv_up, w_out, norm_mlp, w_mlp_in, w_mlp_out, norm_ple,
           w_ple_gate, w_ple_proj):
    batch, seq, d_model = x.shape
    depth = p.shape[0]
    x2d = x.reshape(batch * seq, d_model)
    for i in range(depth):
        x2d = _layer(x2d, p[i].reshape(batch * seq, -1), batch, seq, norm_mix[i], w_in[i], q_gain[i], k_gain[i],
                     rel_bias, w_attn_up[i], shift_mix[i], w0[i], w_decay_up[i], a0[i], w_aaa_up[i],
                     w_gate_up[i], k_k[i], k_a[i], r_k[i], gn_w[i], gn_b[i], w_rwkv_up[i], w_out[i],
                     norm_mlp[i], w_mlp_in[i], w_mlp_out[i], norm_ple[i], w_ple_gate[i], w_ple_proj[i])
    return x2d.reshape(batch, seq, d_model)
```

```python
import functools
import math

import jax
import jax.numpy as jnp
from jax import lax
from jax.experimental import pallas as pl
from jax.experimental.pallas import tpu as pltpu

F32 = jnp.float32
BF16 = jnp.bfloat16

LANES = 128
ATTN_HEAD_DIM = 128
ATTN_HEADS_PER_GROUP = 4
DILATED_GROUPS = ((128, 1), (512, 4), (2048, 16))
ATTN_BLOCK = 128
N_BUCKETS = 32
MAX_DISTANCE = 2048
RWKV_HEAD_DIM = 64
DECAY_LORA = 96
AAA_LORA = 96
GATE_LORA = 256
RMS_EPS = 1e-6
GN_EPS = 64e-5
MASK_VALUE = -1e30

RWKV_CHUNK = 64
RWKV_TBLK = 256
RWKV_HPB = 8

VMEM_LIMIT = 56 * 1024 * 1024


def _cparams(sem):
    return pltpu.CompilerParams(dimension_semantics=sem, vmem_limit_bytes=VMEM_LIMIT)


def _dot(a, b):
    return jnp.dot(a, b, preferred_element_type=F32)


def _dot_nt(a, b):
    return lax.dot_general(a, b, (((1,), (1,)), ((), ())), preferred_element_type=F32)


def _dot_split2(x, w_bf16):
    hi = x.astype(BF16)
    lo = (x - hi.astype(F32)).astype(BF16)
    return _dot(hi, w_bf16) + _dot(lo, w_bf16)


def _dot_split3_left(w_bf16, x):
    hi = x.astype(BF16)
    r1 = x - hi.astype(F32)
    mid = r1.astype(BF16)
    lo = (r1 - mid.astype(F32)).astype(BF16)
    return _dot(w_bf16, hi) + _dot(w_bf16, mid) + _dot(w_bf16, lo)


def _rmsnorm_kernel(x_ref, g_ref, o_ref):
    x = x_ref[...]
    ms = jnp.mean(x * x, axis=-1, keepdims=True)
    o_ref[...] = (x * lax.rsqrt(ms + RMS_EPS) * g_ref[...]).astype(o_ref.dtype)


def _rmsnorm_bf16(x, gain, tr=256):
    m, d = x.shape
    return pl.pallas_call(
        _rmsnorm_kernel,
        grid=(m // tr,),
        in_specs=[pl.BlockSpec((tr, d), lambda i: (i, 0)), pl.BlockSpec((1, d), lambda i: (0, 0))],
        out_specs=pl.BlockSpec((tr, d), lambda i: (i, 0)),
        out_shape=jax.ShapeDtypeStruct((m, d), BF16),
        compiler_params=_cparams(("parallel",)),
        name="rmsnorm",
    )(x, gain.reshape(1, d))


def _qkv_kernel(a_ref, w_ref, g_ref, o_ref, scr_ref, *, dilation):
    j = pl.program_id(1)
    acc = _dot(a_ref[...], w_ref[...])
    rows = acc.shape[0] // dilation
    for h in range(ATTN_HEADS_PER_GROUP):
        y = acc[:, h * LANES:(h + 1) * LANES]
        ms = jnp.mean(y * y, axis=-1, keepdims=True)
        y = y * jnp.where(j < 2, lax.rsqrt(ms + RMS_EPS), 1.0) * g_ref[...]
        if dilation == 1:
            o_ref[h] = y.astype(o_ref.dtype)
        else:
            scr_ref[h] = y
            for r in range(dilation):
                o_ref[h, :, r * LANES:(r + 1) * LANES] = (
                    scr_ref[h, pl.ds(r, rows, stride=dilation), :].astype(o_ref.dtype))


def _qkv_proj(h, w, gains, dilation, batch, seq, tm=1024):
    m, d = h.shape
    tn = ATTN_HEADS_PER_GROUP * LANES
    mt_per_b = seq // tm
    hg = ATTN_HEADS_PER_GROUP
    return pl.pallas_call(
        functools.partial(_qkv_kernel, dilation=dilation),
        grid=(m // tm, 3),
        in_specs=[
            pl.BlockSpec((tm, d), lambda i, j: (i, 0)),
            pl.BlockSpec((d, tn), lambda i, j: (0, j)),
            pl.BlockSpec((None, 1, LANES), lambda i, j: (j, 0, 0)),
        ],
        out_specs=pl.BlockSpec((None, hg, tm // dilation, dilation * LANES),
                               lambda i, j: (i // mt_per_b, j, i % mt_per_b, 0)),
        out_shape=jax.ShapeDtypeStruct((batch, 3 * hg, seq // dilation, dilation * LANES), BF16),
        scratch_shapes=[pltpu.VMEM((hg, tm, LANES), F32)],
        compiler_params=_cparams(("parallel", "arbitrary")),
        name=f"w_in_qkv_d{dilation}",
    )(h, w, gains)


def _mm_kernel(a_ref, w_ref, o_ref, *, act):
    acc = _dot(a_ref[...], w_ref[...])
    if act == "sigmoid":
        acc = jax.nn.sigmoid(acc)
    elif act == "relu2":
        acc = jnp.square(jnp.maximum(acc, 0.0))
    o_ref[...] = acc.astype(o_ref.dtype)


def _matmul(a, w, *, act=None, out_dtype=F32, tm=1024, tn=512, name="matmul"):
    m, k = a.shape
    n = w.shape[1]
    tn = min(tn, n)
    return pl.pallas_call(
        functools.partial(_mm_kernel, act=act),
        grid=(m // tm, n // tn),
        in_specs=[pl.BlockSpec((tm, k), lambda i, j: (i, 0)), pl.BlockSpec((k, tn), lambda i, j: (0, j))],
        out_specs=pl.BlockSpec((tm, tn), lambda i, j: (i, j)),
        out_shape=jax.ShapeDtypeStruct((m, n), out_dtype),
        compiler_params=_cparams(("parallel", "arbitrary")),
        name=name,
    )(a, w)


def _mm_res_kernel(a_ref, w_ref, r_ref, o_ref):
    o_ref[...] = r_ref[...] + _dot(a_ref[...], w_ref[...])


def _matmul_residual(a, w, res, *, tm=1024, tn=512, name="matmul_res"):
    m, k = a.shape
    n = w.shape[1]
    return pl.pallas_call(
        _mm_res_kernel,
        grid=(m // tm, n // tn),
        in_specs=[
            pl.BlockSpec((tm, k), lambda i, j: (i, 0)),
            pl.BlockSpec((k, tn), lambda i, j: (0, j)),
            pl.BlockSpec((tm, tn), lambda i, j: (i, j)),
        ],
        out_specs=pl.BlockSpec((tm, tn), lambda i, j: (i, j)),
        out_shape=jax.ShapeDtypeStruct((m, n), F32),
        compiler_params=_cparams(("parallel", "arbitrary")),
        name=name,
    )(a, w, res)


def _mm_kacc_res_kernel(a_ref, w_ref, r_ref, o_ref, acc_ref):
    k = pl.program_id(2)

    @pl.when(k == 0)
    def _():
        acc_ref[...] = r_ref[...]

    acc_ref[...] += _dot(a_ref[...], w_ref[...])

    @pl.when(k == pl.num_programs(2) - 1)
    def _():
        o_ref[...] = acc_ref[...]


def _matmul_kacc_residual(a, w, res, *, tm=1024, tn=1024, tk=2048, name="matmul_kacc"):
    m, kdim = a.shape
    n = w.shape[1]
    return pl.pallas_call(
        _mm_kacc_res_kernel,
        grid=(m // tm, n // tn, kdim // tk),
        in_specs=[
            pl.BlockSpec((tm, tk), lambda i, j, k: (i, k)),
            pl.BlockSpec((tk, tn), lambda i, j, k: (k, j)),
            pl.BlockSpec((tm, tn), lambda i, j, k: (i, j)),
        ],
        out_specs=pl.BlockSpec((tm, tn), lambda i, j, k: (i, j)),
        out_shape=jax.ShapeDtypeStruct((m, n), F32),
        scratch_shapes=[pltpu.VMEM((tm, tn), F32)],
        compiler_params=_cparams(("parallel", "parallel", "arbitrary")),
        name=name,
    )(a, w, res)


def _t5_bucket(dist):
    max_exact = N_BUCKETS // 2
    d_f = jnp.maximum(dist, 1).astype(F32)
    large = max_exact + (jnp.log(d_f / max_exact) / math.log(MAX_DISTANCE / max_exact)
                         * (N_BUCKETS - max_exact)).astype(jnp.int32)
    large = jnp.minimum(large, N_BUCKETS - 1)
    return jnp.where(dist < max_exact, dist, large)


def _band_buckets():
    blk = ATTN_BLOCK
    rel = (blk + jnp.arange(blk))[:, None] - jnp.arange(2 * blk)[None, :]
    out = []
    for window, dilation in DILATED_GROUPS:
        band = (rel >= 0) & (rel <= window // dilation)
        out.append(jnp.where(band, _t5_bucket(jnp.maximum(rel, 0) * dilation), -1))
    return jnp.stack(out).astype(jnp.int32)


def _attn_kernel(q0, k0, v0, q1, k1, v1, q2, k2, v2, bucket_ref, tab_ref, o_ref, oacc_ref, lacc_ref):
    blk = ATTN_BLOCK
    hh = pl.program_id(1)
    seq = oacc_ref.shape[0]
    refs = ((q0, k0, v0), (q1, k1, v1), (q2, k2, v2))

    def softmax_out(s, v):
        m = jnp.max(s, axis=-1, keepdims=True)
        e = jnp.exp(s - m)
        l = jnp.sum(e, axis=-1, keepdims=True)
        o = _dot(e.astype(BF16), v) / l
        return o, jnp.broadcast_to(m + jnp.log(l), (blk, LANES))

    for g, (_, dilation) in enumerate(DILATED_GROUPS):
        q_ref, k_ref, v_ref = refs[g]
        n_blocks = seq // (dilation * blk)
        bucket = bucket_ref[g]
        head = g * ATTN_HEADS_PER_GROUP + hh
        bias = jnp.where(bucket < 0, MASK_VALUE, 0.0)
        for b in range(N_BUCKETS):
            bias = jnp.where(bucket == b, tab_ref[b, head], bias)

        def merge_store(tok_rows, o, lse, g=g):
            if g == 0:
                oacc_ref[tok_rows, :] = o
                lacc_ref[tok_rows, :] = lse
            else:
                o_old = oacc_ref[tok_rows, :]
                l_old = lacc_ref[tok_rows, :]
                m = jnp.maximum(l_old, lse)
                w_old = jnp.exp(l_old - m)
                w_new = jnp.exp(lse - m)
                den = w_old + w_new
                oacc_ref[tok_rows, :] = (w_old * o_old + w_new * o) / den
                lacc_ref[tok_rows, :] = m + jnp.log(den)

        def tok_rows(off, r, dilation=dilation):
            if dilation == 1:
                return pl.ds(off, blk)
            return pl.ds(off * dilation + r, blk, stride=dilation)

        for r in range(dilation):
            cs = slice(r * LANES, (r + 1) * LANES)
            s0 = _dot_nt(q_ref[0:blk, cs], k_ref[0:blk, cs]) + bias[:, blk:]
            o0, lse0 = softmax_out(s0, v_ref[0:blk, cs])
            merge_store(tok_rows(0, r), o0, lse0)

            def body(n, carry, cs=cs, r=r, q_ref=q_ref, k_ref=k_ref, v_ref=v_ref, bias=bias,
                     merge_store=merge_store, tok_rows=tok_rows):
                off = pl.multiple_of(n * blk, blk)
                s = _dot_nt(q_ref[pl.ds(off, blk), cs], k_ref[pl.ds(off - blk, 2 * blk), cs]) + bias
                o, lse = softmax_out(s, v_ref[pl.ds(off - blk, 2 * blk), cs])
                merge_store(tok_rows(off, r), o, lse)
                return carry

            lax.fori_loop(1, n_blocks, body, 0)

    o_ref[...] = oacc_ref[...].astype(o_ref.dtype)


def _dilated_attention(qkv_groups, rel_bias, batch, seq):
    hg = ATTN_HEADS_PER_GROUP
    in_specs, operands = [], []
    for (_, dilation), arr in zip(DILATED_GROUPS, qkv_groups):
        for section in range(3):
            in_specs.append(pl.BlockSpec((None, None, seq // dilation, dilation * LANES),
                                         lambda b, h, section=section: (b, section * hg + h, 0, 0)))
            operands.append(arr)
    buckets = _band_buckets()
    in_specs.append(pl.BlockSpec(buckets.shape, lambda b, h: (0, 0, 0)))
    in_specs.append(pl.BlockSpec(memory_space=pltpu.SMEM))
    return pl.pallas_call(
        _attn_kernel,
        grid=(batch, hg),
        in_specs=in_specs,
        out_specs=pl.BlockSpec((seq, LANES), lambda b, h: (b, h)),
        out_shape=jax.ShapeDtypeStruct((batch * seq, hg * LANES), BF16),
        scratch_shapes=[pltpu.VMEM((seq, LANES), F32), pltpu.VMEM((seq, LANES), F32)],
        compiler_params=_cparams(("parallel", "parallel")),
        name="dilated_attn",
    )(*operands, buckets, rel_bias.astype(F32))


def _rwkv_kernel(r_ref, k_ref, v_ref, zl_ref, mr_ref, mk_ref, mv_ref, ml_ref, wl_ref,
                 w0_ref, a0_ref, kk_ref, ka_ref, rk_ref, gw_ref, gb_ref, seg_ref, tri_ref,
                 o_ref,
                 state_ref, prev_ref, prevl_ref, at_ref, rt_ref, bt_ref, kt_ref, bh_ref, kh_ref,
                 vb_ref, egc_ref, y_ref, *, lora_dims):
    tblk, cw = r_ref.shape
    ch = RWKV_CHUNK
    n_chunks = tblk // ch
    n_pairs = cw // LANES
    hd = RWKV_HEAD_DIM
    t = pl.program_id(2)

    @pl.when(t == 0)
    def _():
        state_ref[...] = jnp.zeros_like(state_ref)
        prev_ref[...] = jnp.zeros_like(prev_ref)
        prevl_ref[...] = jnp.zeros_like(prevl_ref)

    row = lax.broadcasted_iota(jnp.int32, (tblk, 1), 0)

    def shift(z, prev_row, mix):
        prev = jnp.where(row == 0, prev_row, pltpu.roll(z, 1, axis=0))
        return z + mix * (prev - z)

    zr = r_ref[...]
    zk = k_ref[...]
    zv = v_ref[...]
    zl = zl_ref[...]
    r = shift(zr, prev_ref[0:1, :], mr_ref[...])
    k = shift(zk, prev_ref[1:2, :], mk_ref[...])
    v = shift(zv, prev_ref[2:3, :], mv_ref[...])
    xl = shift(zl, prevl_ref[0:1, :], ml_ref[...])
    prev_ref[0:1, :] = zr[tblk - 1:tblk, :]
    prev_ref[1:2, :] = zk[tblk - 1:tblk, :]
    prev_ref[2:3, :] = zv[tblk - 1:tblk, :]
    prevl_ref[0:1, :] = zl[tblk - 1:tblk, :]

    d_w, d_a, _ = lora_dims
    lane_l = lax.broadcasted_iota(jnp.int32, xl.shape, 1)
    act = jnp.where(lane_l < d_w, jnp.tanh(xl), jnp.where(lane_l < d_w + d_a, xl, jax.nn.sigmoid(xl)))
    up = _dot(act.astype(BF16), wl_ref[...])
    w_arg = -(w0_ref[...] + up[:, 0:cw])
    softplus = jnp.maximum(w_arg, 0.0) + jnp.log1p(jnp.exp(-jnp.abs(w_arg)))
    logdecay = -jnp.exp(-softplus - 0.5)
    a_gate = jax.nn.sigmoid(a0_ref[...] + up[:, cw:2 * cw])
    g_gate = up[:, 2 * cw:3 * cw]

    seg = seg_ref[...]
    kk = k * kk_ref[...]
    kk_norm = jnp.sqrt(_dot_split2(kk * kk, seg))
    kk = kk / jnp.maximum(kk_norm, 1e-12)
    k2 = k * (1.0 + (a_gate - 1.0) * ka_ref[...])
    a_vec = -kk
    b_vec = kk * a_gate
    bonus = _dot_split2(r * k2 * rk_ref[...], seg) * v

    tri = tri_ref[...]
    for c in range(n_chunks):
        rows = slice(c * ch, (c + 1) * ch)
        ld = logdecay[rows]
        g_inc = _dot_split3_left(tri, ld)
        g_exc = g_inc - ld
        g_last = g_inc[ch - 1:ch, :]
        e_neg = jnp.exp(-g_inc)
        e_tail = jnp.exp(g_last - g_inc)
        at_ref[rows, :] = (a_vec[rows] * jnp.exp(g_exc)).astype(BF16)
        rt_ref[rows, :] = (r[rows] * jnp.exp(g_inc)).astype(BF16)
        bt_ref[rows, :] = (b_vec[rows] * e_neg).astype(BF16)
        kt_ref[rows, :] = (k2[rows] * e_neg).astype(BF16)
        bh_ref[rows, :] = (b_vec[rows] * e_tail).astype(BF16)
        kh_ref[rows, :] = (k2[rows] * e_tail).astype(BF16)
        egc_ref[c] = jnp.broadcast_to(jnp.exp(g_last), egc_ref.shape[1:])
    vb_ref[...] = v.astype(BF16)

    lane = lax.broadcasted_iota(jnp.int32, (1, LANES), 1)
    head_masks = (lane < hd, lane >= hd)
    rr = lax.broadcasted_iota(jnp.int32, (ch, 2 * ch), 0)
    cc = lax.broadcasted_iota(jnp.int32, (ch, 2 * ch), 1)
    cc = jnp.where(cc >= ch, cc - ch, cc)
    strict_lower = cc < rr
    incl_lower = cc <= rr
    eye = (lax.broadcasted_iota(jnp.int32, (ch, ch), 0)
           == lax.broadcasted_iota(jnp.int32, (ch, ch), 1)).astype(F32)
    bd_r = lax.broadcasted_iota(jnp.int32, (LANES, LANES), 0) >= hd
    bd_c = lax.broadcasted_iota(jnp.int32, (LANES, LANES), 1) >= hd
    block_diag = bd_r == bd_c
    zeros_cv = jnp.zeros((ch, LANES), BF16)
    lane2 = lax.broadcasted_iota(jnp.int32, (1, 2 * LANES), 1)
    first_head2 = (lane2 % LANES) < hd

    problems = [(c, p) for c in range(n_chunks) for p in range(n_pairs)]
    ld_ = {}
    for (c, p) in problems:
        rows = slice(c * ch, (c + 1) * ch)
        cs = slice(p * LANES, (p + 1) * LANES)
        at, rt, vb = at_ref[rows, cs], rt_ref[rows, cs], vb_ref[rows, cs]
        ld_[c, p] = dict(
            at=at, rt=rt, vb=vb,
            x=jnp.concatenate([at, rt], axis=0),
            ybk=jnp.concatenate([bt_ref[rows, cs], kt_ref[rows, cs]], axis=0),
            bkh=jnp.concatenate([bh_ref[rows, cs], kh_ref[rows, cs]], axis=0),
            zero_v=jnp.concatenate([zeros_cv, vb], axis=0),
        )
    heads = [(c, p, h) for (c, p) in problems for h in range(2)]
    g1, g2, pw, tinv = {}, {}, {}, {}
    for (c, p, h) in heads:
        d = ld_[c, p]
        g_all = _dot_nt(d["x"], jnp.where(head_masks[h], d["ybk"], jnp.zeros_like(d["ybk"])))
        g1[c, p, h] = jnp.where(strict_lower, g_all[:ch], 0.0)
        g2[c, p, h] = jnp.where(incl_lower, g_all[ch:], 0.0).astype(BF16)
        pw[c, p, h] = g1[c, p, h][:, :ch]
        tinv[c, p, h] = eye + pw[c, p, h]
    for _ in range(int(math.log2(ch)) - 1):
        for q in heads:
            pb = pw[q].astype(BF16)
            pw[q] = _dot(pb, pb)
        for q in heads:
            tinv[q] = tinv[q] + _dot(tinv[q].astype(BF16), pw[q].astype(BF16))
    wu = {}
    for (c, p, h) in heads:
        d = ld_[c, p]
        aakv = _dot(g1[c, p, h].astype(BF16), d["zero_v"])
        rhs = jnp.concatenate([d["at"], aakv.astype(BF16)], axis=1)
        wu[c, p, h] = _dot(tinv[c, p, h].astype(BF16), rhs)
    ry = {}
    for (c, p, h) in heads:
        d = ld_[c, p]
        lower = jnp.concatenate([zeros_cv, d["vb"]], axis=1)
        rhs = jnp.concatenate([wu[c, p, h].astype(BF16), lower], axis=0)
        ry[c, p, h] = _dot(g2[c, p, h], rhs)
    rt2, y0, mlr, n0 = {}, {}, {}, {}
    for (c, p) in problems:
        d = ld_[c, p]
        wu_pair = jnp.where(first_head2, wu[c, p, 0], wu[c, p, 1])
        ry_pair = jnp.where(first_head2, ry[c, p, 0], ry[c, p, 1])
        rt2[c, p] = (d["rt"].astype(F32) + ry_pair[:, :LANES]).astype(BF16)
        y0[c, p] = ry_pair[:, LANES:]
        wt_t = wu_pair[:, :LANES].T.astype(BF16)
        mlr[c, p] = jnp.where(block_diag, _dot(wt_t, d["bkh"][:ch]), 0.0).astype(BF16)
        uv_t = jnp.concatenate([wu_pair[:, LANES:], d["vb"].astype(F32)], axis=0).T.astype(BF16)
        n0[c, p] = jnp.where(block_diag, _dot(uv_t, d["bkh"]), 0.0)

    for c in range(n_chunks):
        rows = slice(c * ch, (c + 1) * ch)
        for p in range(n_pairs):
            cs = slice(p * LANES, (p + 1) * LANES)
            ht = state_ref[p]
            htb = ht.astype(BF16)
            y_ref[rows, cs] = _dot_nt(rt2[c, p], htb) + y0[c, p]
            state_ref[p] = ht * egc_ref[c, 0:1, cs] + _dot(htb, mlr[c, p]) + n0[c, p]

    y = y_ref[...]
    inv_hd = 1.0 / hd
    mu = _dot_split2(y, seg) * inv_hd
    yc = y - mu
    var = _dot_split2(yc * yc, seg) * inv_hd
    yn = yc * lax.rsqrt(var + GN_EPS) * gw_ref[...] + gb_ref[...]
    o_ref[...] = ((yn + bonus) * g_gate).astype(o_ref.dtype)


def _rwkv7(zr, zl, shift_mix, w0, w_decay_up, a0, w_aaa_up, w_gate_up, k_k, k_a, r_k, gn_w, gn_b,
           batch, seq):
    tok = zr.shape[0]
    width = zr.shape[1] // 3
    lw = zl.shape[1]
    cw = RWKV_HPB * RWKV_HEAD_DIM
    nhb = width // cw
    tblk = RWKV_TBLK
    nt = seq // tblk
    ch = RWKV_CHUNK

    def pad_rows(w, start):
        return jnp.pad(w, ((start, lw - start - w.shape[0]), (0, 0)))

    wl = jnp.stack([pad_rows(w_decay_up, 0), pad_rows(w_aaa_up, DECAY_LORA),
                    pad_rows(w_gate_up, DECAY_LORA + AAA_LORA)], axis=0)
    wl = wl.reshape(3, lw, nhb, cw).transpose(2, 1, 0, 3).reshape(nhb, lw, 3 * cw).astype(BF16)

    mix_z = shift_mix[: 3 * width].reshape(1, 3 * width)
    mix_l = shift_mix[3 * width:].reshape(1, lw)
    idx = jnp.arange(cw) // RWKV_HEAD_DIM
    seg = (idx[:, None] == idx[None, :]).astype(BF16)
    tri = (jnp.arange(ch)[:, None] >= jnp.arange(ch)[None, :]).astype(BF16)

    def col(sec):
        return pl.BlockSpec((tblk, cw), lambda b, h, t: (b * nt + t, sec * nhb + h))

    def mixcol(sec):
        return pl.BlockSpec((1, cw), lambda b, h, t: (0, sec * nhb + h))

    vec = pl.BlockSpec((1, cw), lambda b, h, t: (0, h))
    row_vec = lambda a: a.reshape(1, width)
    kern = functools.partial(_rwkv_kernel, lora_dims=(DECAY_LORA, AAA_LORA, GATE_LORA))
    return pl.pallas_call(
        kern,
        grid=(batch, nhb, nt),
        in_specs=[
            col(0), col(1), col(2),
            pl.BlockSpec((tblk, lw), lambda b, h, t: (b * nt + t, 0)),
            mixcol(0), mixcol(1), mixcol(2),
            pl.BlockSpec((1, lw), lambda b, h, t: (0, 0)),
            pl.BlockSpec((None, lw, 3 * cw), lambda b, h, t: (h, 0, 0)),
            vec, vec, vec, vec, vec, vec, vec,
            pl.BlockSpec((cw, cw), lambda b, h, t: (0, 0)),
            pl.BlockSpec((ch, ch), lambda b, h, t: (0, 0)),
        ],
        out_specs=pl.BlockSpec((tblk, cw), lambda b, h, t: (b * nt + t, h)),
        out_shape=jax.ShapeDtypeStruct((tok, width), BF16),
        scratch_shapes=[
            pltpu.VMEM((cw // LANES, LANES, LANES), F32),
            pltpu.VMEM((8, cw), F32),
            pltpu.VMEM((8, lw), F32),
            pltpu.VMEM((tblk, cw), BF16), pltpu.VMEM((tblk, cw), BF16), pltpu.VMEM((tblk, cw), BF16),
            pltpu.VMEM((tblk, cw), BF16), pltpu.VMEM((tblk, cw), BF16), pltpu.VMEM((tblk, cw), BF16),
            pltpu.VMEM((tblk, cw), BF16),
            pltpu.VMEM((tblk // ch, 8, cw), F32),
            pltpu.VMEM((tblk, cw), F32),
        ],
        compiler_params=_cparams(("parallel", "parallel", "arbitrary")),
        name="rwkv7_chunked",
    )(zr, zr, zr, zl, mix_z, mix_z, mix_z, mix_l, wl,
      row_vec(w0), row_vec(a0), row_vec(k_k), row_vec(k_a), row_vec(r_k), row_vec(gn_w), row_vec(gn_b),
      seg, tri)


def _merge_kernel(attn_ref, rw_ref, g0_ref, g1_ref, wa_ref, wr_ref, out_ref):
    attn_d = _dot(attn_ref[...], wa_ref[...])
    rwkv_d = _dot(rw_ref[...], wr_ref[...])
    out_ref[...] = (g0_ref[...].astype(F32) * attn_d + g1_ref[...].astype(F32) * rwkv_d).astype(out_ref.dtype)


def _merge(attn, rw, gates, w_attn_up, w_rwkv_up, tm=1024, tn=512):
    tok = rw.shape[0]
    d = w_attn_up.shape[1]
    nj = d // tn
    return pl.pallas_call(
        _merge_kernel,
        grid=(tok // tm, nj),
        in_specs=[
            pl.BlockSpec((tm, attn.shape[1]), lambda i, j: (i, 0)),
            pl.BlockSpec((tm, rw.shape[1]), lambda i, j: (i, 0)),
            pl.BlockSpec((tm, tn), lambda i, j: (i, j)),
            pl.BlockSpec((tm, tn), lambda i, j: (i, nj + j)),
            pl.BlockSpec((w_attn_up.shape[0], tn), lambda i, j: (0, j)),
            pl.BlockSpec((w_rwkv_up.shape[0], tn), lambda i, j: (0, j)),
        ],
        out_specs=pl.BlockSpec((tm, tn), lambda i, j: (i, j)),
        out_shape=jax.ShapeDtypeStruct((tok, d), BF16),
        compiler_params=_cparams(("parallel", "arbitrary")),
        name="merge_gated_up",
    )(attn, rw, gates, gates, w_attn_up, w_rwkv_up)


def _ple_kernel(h_ref, wg_ref, p_ref, wp_ref, x_ref, o_ref):
    gate = jax.nn.sigmoid(_dot(h_ref[...], wg_ref[...]))
    o_ref[...] = x_ref[...] + gate * _dot(p_ref[...], wp_ref[...])


def _ple(hn, w_gate, p, w_proj, x, tm=1024, tn=512):
    m, k = hn.shape
    n = w_gate.shape[1]
    kp = p.shape[1]
    return pl.pallas_call(
        _ple_kernel,
        grid=(m // tm, n // tn),
        in_specs=[
            pl.BlockSpec((tm, k), lambda i, j: (i, 0)),
            pl.BlockSpec((k, tn), lambda i, j: (0, j)),
            pl.BlockSpec((tm, kp), lambda i, j: (i, 0)),
            pl.BlockSpec((kp, tn), lambda i, j: (0, j)),
            pl.BlockSpec((tm, tn), lambda i, j: (i, j)),
        ],
        out_specs=pl.BlockSpec((tm, tn), lambda i, j: (i, j)),
        out_shape=jax.ShapeDtypeStruct((m, n), F32),
        compiler_params=_cparams(("parallel", "arbitrary")),
        name="ple_gated",
    )(hn, w_gate, p, w_proj, x)


def _layer(x2d, p2d, batch, seq, norm_mix, w_in, q_gain, k_gain, rel_bias, w_attn_up, shift_mix, w0,
           w_decay_up, a0, w_aaa_up, w_gate_up, k_k, k_a, r_k, gn_w, gn_b, w_rwkv_up, w_out, norm_mlp,
           w_mlp_in, w_mlp_out, norm_ple, w_ple_gate, w_ple_proj):
    d_model = x2d.shape[1]
    n_attn_heads = len(DILATED_GROUPS) * ATTN_HEADS_PER_GROUP
    attn_width = n_attn_heads * ATTN_HEAD_DIM
    rwkv_width = w_rwkv_up.shape[0]
    a_end = 3 * attn_width
    z_end = a_end + 3 * rwkv_width
    r_end = z_end + DECAY_LORA + AAA_LORA + GATE_LORA

    h = _rmsnorm_bf16(x2d, norm_mix)

    gains = jnp.stack([q_gain * (ATTN_HEAD_DIM ** -0.5), k_gain, jnp.ones_like(q_gain)]).reshape(3, 1, LANES)
    gw = ATTN_HEADS_PER_GROUP * ATTN_HEAD_DIM
    qkv_groups = []
    for gi, (_, dilation) in enumerate(DILATED_GROUPS):
        w_g = jnp.concatenate([w_in[:, sec * attn_width + gi * gw: sec * attn_width + (gi + 1) * gw]
                               for sec in range(3)], axis=1).astype(BF16)
        qkv_groups.append(_qkv_proj(h, w_g, gains, dilation, batch, seq))
    zr = _matmul(h, w_in[:, a_end:z_end].astype(BF16), name="w_in_rkv")
    zl = _matmul(h, w_in[:, z_end:r_end].astype(BF16), name="w_in_lora")
    gates = _matmul(h, w_in[:, r_end:].astype(BF16), act="sigmoid", out_dtype=BF16, name="w_in_gates")

    attn = _dilated_attention(qkv_groups, rel_bias, batch, seq)

    rw = _rwkv7(zr, zl, shift_mix, w0, w_decay_up, a0, w_aaa_up, w_gate_up, k_k, k_a, r_k, gn_w, gn_b,
                batch, seq)

    merged = _merge(attn, rw, gates, w_attn_up.astype(BF16), w_rwkv_up.astype(BF16))
    x1 = _matmul_residual(merged, w_out.astype(BF16), x2d, name="w_out_res")

    h2 = _rmsnorm_bf16(x1, norm_mlp)
    u = _matmul(h2, w_mlp_in.astype(BF16), act="relu2", out_dtype=BF16, name="mlp_in")
    x2 = _matmul_kacc_residual(u, w_mlp_out.astype(BF16), x1, name="mlp_out_res")

    hn = _rmsnorm_bf16(x2, norm_ple)
    del d_model
    return _ple(hn, w_ple_gate.astype(BF16), p2d.astype(BF16), w_ple_proj.astype(BF16), x2)


def kernel(x, p, norm_mix, w_in, q_gain, k_gain, rel_bias, w_attn_up, shift_mix, w0, w_decay_up, a0, w_aaa_up,
           w_gate_up, k_k, k_a, r_k, gn_w, gn_b, w_rwkv_up, w_out, norm_mlp, w_mlp_in, w_mlp_out, norm_ple,
           w_ple_gate, w_ple_proj):
    batch, seq, d_model = x.shape
    depth = p.shape[0]
    x2d = x.reshape(batch * seq, d_model)
    for i in range(depth):
        x2d = _layer(x2d, p[i].reshape(batch * seq, -1), batch, seq, norm_mix[i], w_in[i], q_gain[i], k_gain[i],
                     rel_bias, w_attn_up[i], shift_mix[i], w0[i], w_decay_up[i], a0[i], w_aaa_up[i],
                     w_gate_up[i], k_k[i], k_a[i], r_k[i], gn_w[i], gn_b[i], w_rwkv_up[i], w_out[i],
                     norm_mlp[i], w_mlp_in[i], w_mlp_out[i], norm_ple[i], w_ple_gate[i], w_ple_proj[i])
    return x2d.reshape(batch, seq, d_model)
```

```python
import functools
import math

import jax
import jax.numpy as jnp
from jax import lax
from jax.experimental import pallas as pl
from jax.experimental.pallas import tpu as pltpu

F32 = jnp.float32
BF16 = jnp.bfloat16

LANES = 128
MXU_DIM = 256
ATTN_HEAD_DIM = 128
ATTN_HEADS_PER_GROUP = 4
DILATED_GROUPS = ((128, 1), (512, 4), (2048, 16))
ATTN_BLOCK = 128
N_BUCKETS = 32
MAX_DISTANCE = 2048
RWKV_HEAD_DIM = 64
DECAY_LORA = 96
AAA_LORA = 96
GATE_LORA = 256
RMS_EPS = 1e-6
GN_EPS = 64e-5
MASK_VALUE = -1e30

RWKV_CHUNK = 64
RWKV_TBLK = 256
RWKV_HPB = 8

VMEM_LIMIT = 56 * 1024 * 1024


def _cparams(sem):
    return pltpu.CompilerParams(dimension_semantics=sem, vmem_limit_bytes=VMEM_LIMIT)


def _dot(a, b):
    return jnp.dot(a, b, preferred_element_type=F32)


def _dot_nt(a, b):
    return lax.dot_general(a, b, (((1,), (1,)), ((), ())), preferred_element_type=F32)


def _dot_split2(x, w_bf16):
    hi = x.astype(BF16)
    lo = (x - hi.astype(F32)).astype(BF16)
    return _dot(hi, w_bf16) + _dot(lo, w_bf16)


def _dot_split3_left(w_bf16, x):
    hi = x.astype(BF16)
    r1 = x - hi.astype(F32)
    mid = r1.astype(BF16)
    lo = (r1 - mid.astype(F32)).astype(BF16)
    return _dot(w_bf16, hi) + _dot(w_bf16, mid) + _dot(w_bf16, lo)


def _rmsnorm_kernel(x_ref, g_ref, o_ref):
    x = x_ref[...]
    ms = jnp.mean(x * x, axis=-1, keepdims=True)
    o_ref[...] = (x * lax.rsqrt(ms + RMS_EPS) * g_ref[...]).astype(o_ref.dtype)


def _rmsnorm_bf16(x, gain, tr=256):
    m, d = x.shape
    return pl.pallas_call(
        _rmsnorm_kernel,
        grid=(m // tr,),
        in_specs=[pl.BlockSpec((tr, d), lambda i: (i, 0)), pl.BlockSpec((1, d), lambda i: (0, 0))],
        out_specs=pl.BlockSpec((tr, d), lambda i: (i, 0)),
        out_shape=jax.ShapeDtypeStruct((m, d), BF16),
        compiler_params=_cparams(("parallel",)),
        name="rmsnorm",
    )(x, gain.reshape(1, d))


def _qkv_kernel(a_ref, w_ref, g_ref, o_ref, scr_ref, *, dilation):
    j = pl.program_id(1)
    acc = _dot(a_ref[...], w_ref[...])
    rows = acc.shape[0] // dilation
    for h in range(ATTN_HEADS_PER_GROUP):
        y = acc[:, h * LANES:(h + 1) * LANES]
        ms = jnp.mean(y * y, axis=-1, keepdims=True)
        y = y * jnp.where(j < 2, lax.rsqrt(ms + RMS_EPS), 1.0) * g_ref[...]
        if dilation == 1:
            o_ref[h] = y.astype(o_ref.dtype)
        else:
            scr_ref[h] = y
            for r in range(dilation):
                o_ref[h, :, r * LANES:(r + 1) * LANES] = (
                    scr_ref[h, pl.ds(r, rows, stride=dilation), :].astype(o_ref.dtype))


def _qkv_proj(h, w, gains, dilation, batch, seq, tm=1024):
    m, d = h.shape
    tn = ATTN_HEADS_PER_GROUP * LANES
    mt_per_b = seq // tm
    hg = ATTN_HEADS_PER_GROUP
    return pl.pallas_call(
        functools.partial(_qkv_kernel, dilation=dilation),
        grid=(m // tm, 3),
        in_specs=[
            pl.BlockSpec((tm, d), lambda i, j: (i, 0)),
            pl.BlockSpec((d, tn), lambda i, j: (0, j)),
            pl.BlockSpec((None, 1, LANES), lambda i, j: (j, 0, 0)),
        ],
        out_specs=pl.BlockSpec((None, hg, tm // dilation, dilation * LANES),
                               lambda i, j: (i // mt_per_b, j, i % mt_per_b, 0)),
        out_shape=jax.ShapeDtypeStruct((batch, 3 * hg, seq // dilation, dilation * LANES), BF16),
        scratch_shapes=[pltpu.VMEM((hg, tm, LANES), F32)],
        compiler_params=_cparams(("parallel", "arbitrary")),
        name=f"w_in_qkv_d{dilation}",
    )(h, w, gains)


def _mm_kernel(a_ref, w_ref, o_ref, *, act):
    acc = _dot(a_ref[...], w_ref[...])
    if act == "sigmoid":
        acc = jax.nn.sigmoid(acc)
    elif act == "relu2":
        acc = jnp.square(jnp.maximum(acc, 0.0))
    o_ref[...] = acc.astype(o_ref.dtype)


def _matmul(a, w, *, act=None, out_dtype=F32, tm=1024, tn=1024, name="matmul"):
    m, k = a.shape
    n = w.shape[1]
    tn = min(tn, n)
    return pl.pallas_call(
        functools.partial(_mm_kernel, act=act),
        grid=(m // tm, n // tn),
        in_specs=[pl.BlockSpec((tm, k), lambda i, j: (i, 0)), pl.BlockSpec((k, tn), lambda i, j: (0, j))],
        out_specs=pl.BlockSpec((tm, tn), lambda i, j: (i, j)),
        out_shape=jax.ShapeDtypeStruct((m, n), out_dtype),
        compiler_params=_cparams(("parallel", "arbitrary")),
        name=name,
    )(a, w)


def _mm_res_kernel(a_ref, w_ref, r_ref, o_ref):
    o_ref[...] = r_ref[...] + _dot(a_ref[...], w_ref[...])


def _matmul_residual(a, w, res, *, tm=1024, tn=1024, name="matmul_res"):
    m, k = a.shape
    n = w.shape[1]
    return pl.pallas_call(
        _mm_res_kernel,
        grid=(m // tm, n // tn),
        in_specs=[
            pl.BlockSpec((tm, k), lambda i, j: (i, 0)),
            pl.BlockSpec((k, tn), lambda i, j: (0, j)),
            pl.BlockSpec((tm, tn), lambda i, j: (i, j)),
        ],
        out_specs=pl.BlockSpec((tm, tn), lambda i, j: (i, j)),
        out_shape=jax.ShapeDtypeStruct((m, n), F32),
        compiler_params=_cparams(("parallel", "arbitrary")),
        name=name,
    )(a, w, res)


def _mm_kacc_res_kernel(a_ref, w_ref, r_ref, o_ref):
    k = pl.program_id(2)

    @pl.when(k == 0)
    def _():
        o_ref[...] = r_ref[...] + _dot(a_ref[...], w_ref[...])

    @pl.when(k > 0)
    def _():
        o_ref[...] += _dot(a_ref[...], w_ref[...])


def _matmul_kacc_residual(a, w, res, *, tm=1024, tn=1024, tk=2048, name="matmul_kacc"):
    m, kdim = a.shape
    n = w.shape[1]
    return pl.pallas_call(
        _mm_kacc_res_kernel,
        grid=(m // tm, n // tn, kdim // tk),
        in_specs=[
            pl.BlockSpec((tm, tk), lambda i, j, k: (i, k)),
            pl.BlockSpec((tk, tn), lambda i, j, k: (k, j)),
            pl.BlockSpec((tm, tn), lambda i, j, k: (i, j)),
        ],
        out_specs=pl.BlockSpec((tm, tn), lambda i, j, k: (i, j)),
        out_shape=jax.ShapeDtypeStruct((m, n), F32),
        compiler_params=_cparams(("parallel", "parallel", "arbitrary")),
        name=name,
    )(a, w, res)


def _t5_bucket(dist):
    max_exact = N_BUCKETS // 2
    d_f = jnp.maximum(dist, 1).astype(F32)
    large = max_exact + (jnp.log(d_f / max_exact) / math.log(MAX_DISTANCE / max_exact)
                         * (N_BUCKETS - max_exact)).astype(jnp.int32)
    large = jnp.minimum(large, N_BUCKETS - 1)
    return jnp.where(dist < max_exact, dist, large)


def _band_buckets():
    blk = ATTN_BLOCK
    rel = (blk + jnp.arange(blk))[:, None] - jnp.arange(2 * blk)[None, :]
    out = []
    for window, dilation in DILATED_GROUPS:
        band = (rel >= 0) & (rel <= window // dilation)
        out.append(jnp.where(band, _t5_bucket(jnp.maximum(rel, 0) * dilation), -1))
    return jnp.stack(out).astype(jnp.int32)


def _attn_kernel(q0, k0, v0, q1, k1, v1, q2, k2, v2, bucket_ref, tab_ref, o_ref, oacc_ref, lacc_ref):
    blk = ATTN_BLOCK
    hh = pl.program_id(1)
    seq = oacc_ref.shape[0]
    refs = ((q0, k0, v0), (q1, k1, v1), (q2, k2, v2))

    def softmax_out(s, v):
        m = jnp.max(s, axis=-1, keepdims=True)
        e = jnp.exp(s - m)
        l = jnp.sum(e, axis=-1, keepdims=True)
        o = _dot(e.astype(BF16), v) / l
        return o, jnp.broadcast_to(m + jnp.log(l), (blk, LANES))

    for g, (_, dilation) in enumerate(DILATED_GROUPS):
        q_ref, k_ref, v_ref = refs[g]
        n_blocks = seq // (dilation * blk)
        bucket = bucket_ref[g]
        head = g * ATTN_HEADS_PER_GROUP + hh
        bias = jnp.where(bucket < 0, MASK_VALUE, 0.0)
        for b in range(N_BUCKETS):
            bias = jnp.where(bucket == b, tab_ref[b, head], bias)

        def merge_store(tok_rows, o, lse, g=g):
            if g == 0:
                oacc_ref[tok_rows, :] = o
                lacc_ref[tok_rows, :] = lse
            else:
                o_old = oacc_ref[tok_rows, :]
                l_old = lacc_ref[tok_rows, :]
                m = jnp.maximum(l_old, lse)
                w_old = jnp.exp(l_old - m)
                w_new = jnp.exp(lse - m)
                den = w_old + w_new
                oacc_ref[tok_rows, :] = (w_old * o_old + w_new * o) / den
                lacc_ref[tok_rows, :] = m + jnp.log(den)

        def tok_rows(off, r, dilation=dilation):
            if dilation == 1:
                return pl.ds(off, blk)
            return pl.ds(off * dilation + r, blk, stride=dilation)

        for r in range(dilation):
            cs = slice(r * LANES, (r + 1) * LANES)
            s0 = _dot_nt(q_ref[0:blk, cs], k_ref[0:blk, cs]) + bias[:, blk:]
            o0, lse0 = softmax_out(s0, v_ref[0:blk, cs])
            merge_store(tok_rows(0, r), o0, lse0)

            def body(n, carry, cs=cs, r=r, q_ref=q_ref, k_ref=k_ref, v_ref=v_ref, bias=bias,
                     merge_store=merge_store, tok_rows=tok_rows):
                off = pl.multiple_of(n * blk, blk)
                s = _dot_nt(q_ref[pl.ds(off, blk), cs], k_ref[pl.ds(off - blk, 2 * blk), cs]) + bias
                o, lse = softmax_out(s, v_ref[pl.ds(off - blk, 2 * blk), cs])
                merge_store(tok_rows(off, r), o, lse)
                return carry

            lax.fori_loop(1, n_blocks, body, 0)

    o_ref[...] = oacc_ref[...].astype(o_ref.dtype)


def _dilated_attention(qkv_groups, rel_bias, batch, seq):
    hg = ATTN_HEADS_PER_GROUP
    in_specs, operands = [], []
    for (_, dilation), arr in zip(DILATED_GROUPS, qkv_groups):
        for section in range(3):
            in_specs.append(pl.BlockSpec((None, None, seq // dilation, dilation * LANES),
                                         lambda b, h, section=section: (b, section * hg + h, 0, 0)))
            operands.append(arr)
    buckets = _band_buckets()
    in_specs.append(pl.BlockSpec(buckets.shape, lambda b, h: (0, 0, 0)))
    in_specs.append(pl.BlockSpec(memory_space=pltpu.SMEM))
    return pl.pallas_call(
        _attn_kernel,
        grid=(batch, hg),
        in_specs=in_specs,
        out_specs=pl.BlockSpec((seq, LANES), lambda b, h: (b, h)),
        out_shape=jax.ShapeDtypeStruct((batch * seq, hg * LANES), BF16),
        scratch_shapes=[pltpu.VMEM((seq, LANES), F32), pltpu.VMEM((seq, LANES), F32)],
        compiler_params=_cparams(("parallel", "parallel")),
        name="dilated_attn",
    )(*operands, buckets, rel_bias.astype(F32))


_STAGE_BF16 = ("at", "rt", "bt", "kt", "bh", "kh", "vb")
_STAGE_F32 = ("egc", "bonus", "gate")
_STAGE_NAMES = _STAGE_BF16 + _STAGE_F32
_N_HANDOFF = 9


def _seg_sum(x, seg):
    w = seg.shape[0]
    return jnp.concatenate([_dot_split2(x[:, i * w:(i + 1) * w], seg) for i in range(x.shape[1] // w)], axis=1)


def _rwkv_prepare(r_ref, k_ref, v_ref, zl_ref, mr_ref, mk_ref, mv_ref, ml_ref, wl_ref, w0_ref, a0_ref, kk_ref,
                  ka_ref, rk_ref, seg_ref, tri_ref, prev_ref, prevl_ref, stage, lora_dims):
    tblk, cw = r_ref.shape
    ch = RWKV_CHUNK
    row = lax.broadcasted_iota(jnp.int32, (tblk, 1), 0)

    def shift(z, prev_row, mix):
        prev = jnp.where(row == 0, prev_row, pltpu.roll(z, 1, axis=0))
        return z + mix * (prev - z)

    zr, zk, zv, zl = r_ref[...], k_ref[...], v_ref[...], zl_ref[...]
    r = shift(zr, prev_ref[0:1, :], mr_ref[...])
    k = shift(zk, prev_ref[1:2, :], mk_ref[...])
    v = shift(zv, prev_ref[2:3, :], mv_ref[...])
    xl = shift(zl, prevl_ref[0:1, :], ml_ref[...])
    prev_ref[0:1, :] = zr[tblk - 1:tblk, :]
    prev_ref[1:2, :] = zk[tblk - 1:tblk, :]
    prev_ref[2:3, :] = zv[tblk - 1:tblk, :]
    prevl_ref[0:1, :] = zl[tblk - 1:tblk, :]

    d_w, d_a, _ = lora_dims
    lane_l = lax.broadcasted_iota(jnp.int32, xl.shape, 1)
    act = jnp.where(lane_l < d_w, jnp.tanh(xl), jnp.where(lane_l < d_w + d_a, xl, jax.nn.sigmoid(xl)))
    up = _dot(act.astype(BF16), wl_ref[...])
    w_arg = -(w0_ref[...] + up[:, 0:cw])
    softplus = jnp.maximum(w_arg, 0.0) + jnp.log1p(jnp.exp(-jnp.abs(w_arg)))
    logdecay = -jnp.exp(-softplus - 0.5)
    a_gate = jax.nn.sigmoid(a0_ref[...] + up[:, cw:2 * cw])
    stage["gate"][...] = up[:, 2 * cw:3 * cw]

    seg = seg_ref[...]
    kk = k * kk_ref[...]
    kk = kk * lax.rsqrt(jnp.maximum(_seg_sum(kk * kk, seg), 1e-24))
    k2 = k * (1.0 + (a_gate - 1.0) * ka_ref[...])
    a_vec = -kk
    b_vec = kk * a_gate
    stage["bonus"][...] = _seg_sum(r * k2 * rk_ref[...], seg) * v

    tri = tri_ref[...]
    for c in range(tblk // ch):
        rows = slice(c * ch, (c + 1) * ch)
        ld = logdecay[rows]
        g_inc = _dot_split3_left(tri, ld)
        g_exc = g_inc - ld
        e_neg = jnp.exp(-g_inc)
        e_last = jnp.exp(g_inc[ch - 1:ch, :])
        bt = b_vec[rows] * e_neg
        kt = k2[rows] * e_neg
        stage["at"][rows, :] = (a_vec[rows] * jnp.exp(g_exc)).astype(BF16)
        stage["rt"][rows, :] = (r[rows] * jnp.exp(g_inc)).astype(BF16)
        stage["bt"][rows, :] = bt.astype(BF16)
        stage["kt"][rows, :] = kt.astype(BF16)
        stage["bh"][rows, :] = (bt * e_last).astype(BF16)
        stage["kh"][rows, :] = (kt * e_last).astype(BF16)
        stage["egc"][c] = jnp.broadcast_to(e_last, stage["egc"].shape[1:])
    stage["vb"][...] = v.astype(BF16)


def _rwkv_solve(stage, state_ref, y_ref, o_ref, gw_ref, gb_ref, seg_ref, handoff, zero_off):
    tblk, cw = o_ref.shape
    ch = RWKV_CHUNK
    n_chunks = tblk // ch
    n_pairs = cw // LANES
    hd = RWKV_HEAD_DIM

    lane = lax.broadcasted_iota(jnp.int32, (1, LANES), 1)
    m0, m1 = lane < hd, lane >= hd
    lane2 = lax.broadcasted_iota(jnp.int32, (1, 2 * LANES), 1) % LANES
    m0w, m1w = lane2 < hd, lane2 >= hd
    rr = lax.broadcasted_iota(jnp.int32, (ch, 2 * LANES), 0)
    cc = lax.broadcasted_iota(jnp.int32, (ch, 2 * LANES), 1) % ch
    strict_lower = cc < rr
    incl_lower = cc <= rr
    eye = (lax.broadcasted_iota(jnp.int32, (LANES, LANES), 0)
           == lax.broadcasted_iota(jnp.int32, (LANES, LANES), 1)).astype(F32)
    bd_r = lax.broadcasted_iota(jnp.int32, (LANES, LANES), 0) >= hd
    bd_c = lax.broadcasted_iota(jnp.int32, (LANES, LANES), 1) >= hd
    block_diag = bd_r == bd_c
    zeros_cv = jnp.zeros((ch, LANES), BF16)

    def keep(mask, x):
        return jnp.where(mask, x, jnp.zeros_like(x))

    problems = [(c, p) for c in range(n_chunks) for p in range(n_pairs)]

    def blk(name, cp):
        c, p = cp
        return stage[name][c * ch:(c + 1) * ch, p * LANES:(p + 1) * LANES]

    aakv_s, tcat_s, g2_s, wu_s, ry_s, rt2_s, y0_s, mlr_s, n0_s = handoff

    g1, pw, tinv = {}, {}, {}
    for i, cp in enumerate(problems):
        bt, kt = blk("bt", cp), blk("kt", cp)
        x = jnp.concatenate([blk("at", cp), blk("rt", cp)], axis=0)
        ym = jnp.concatenate([keep(m0, bt), keep(m0, kt), keep(m1, kt), keep(m1, bt)], axis=0)
        g_all = _dot_nt(x, ym)
        g1[cp] = jnp.where(strict_lower, g_all[:ch], 0.0)
        g2_s[i] = jnp.where(incl_lower, g_all[ch:], 0.0).astype(BF16)
        pw[cp] = jnp.concatenate([keep(m0, g1[cp][:, :LANES]), keep(m1, g1[cp][:, LANES:])], axis=0)
        tinv[cp] = eye + pw[cp]
    for _ in range(int(math.log2(ch)) - 1):
        for cp in problems:
            pb = pw[cp].astype(BF16)
            pw[cp] = _dot(pb, pb)
        for cp in problems:
            tinv[cp] = tinv[cp] + _dot(tinv[cp].astype(BF16), pw[cp].astype(BF16))
    for i, cp in enumerate(problems):
        vb = blk("vb", cp)
        v_stack = jnp.concatenate([zeros_cv, keep(m0, vb), keep(m1, vb), zeros_cv], axis=0)
        aakv_s[i] = _dot(g1[cp].astype(BF16), v_stack).astype(BF16)
        tcat_s[i] = (tinv[cp][:ch] + tinv[cp][ch:]).astype(BF16)
    for i, cp in enumerate(problems):
        j = zero_off + i
        rhs = jnp.concatenate([blk("at", cp), aakv_s[j]], axis=1)
        wu_s[i] = _dot(tcat_s[j], jnp.concatenate([keep(m0w, rhs), keep(m1w, rhs)], axis=0))
    for i, cp in enumerate(problems):
        j = zero_off + i
        upper = wu_s[j].astype(BF16)
        lower = jnp.concatenate([zeros_cv, blk("vb", cp)], axis=1)
        rhs = jnp.concatenate([keep(m0w, upper), keep(m0w, lower), keep(m1w, lower), keep(m1w, upper)], axis=0)
        ry_s[i] = _dot(g2_s[j], rhs)
    for i, cp in enumerate(problems):
        j = zero_off + i
        wu, ry = wu_s[j], ry_s[j]
        bkh = jnp.concatenate([blk("bh", cp), blk("kh", cp)], axis=0)
        rt2_s[i] = (blk("rt", cp).astype(F32) + ry[:, :LANES]).astype(BF16)
        y0_s[i] = ry[:, LANES:]
        wt_t = wu[:, :LANES].T.astype(BF16)
        mlr_s[i] = jnp.where(block_diag, _dot(wt_t, bkh[:ch]), 0.0).astype(BF16)
        uv_t = jnp.concatenate([wu[:, LANES:], blk("vb", cp).astype(F32)], axis=0).T.astype(BF16)
        n0_s[i] = jnp.where(block_diag, _dot(uv_t, bkh), 0.0)

    for c in range(n_chunks):
        rows = slice(c * ch, (c + 1) * ch)
        for p in range(n_pairs):
            cs = slice(p * LANES, (p + 1) * LANES)
            j = zero_off + (c * n_pairs + p)
            ht = state_ref[p]
            htb = ht.astype(BF16)
            y_ref[rows, cs] = _dot_nt(rt2_s[j], htb) + y0_s[j]
            state_ref[p] = ht * stage["egc"][c, 0:1, cs] + _dot(htb, mlr_s[j]) + n0_s[j]

    seg = seg_ref[...]
    y = y_ref[...]
    inv_hd = 1.0 / hd
    mu = _seg_sum(y, seg) * inv_hd
    yc = y - mu
    var = _seg_sum(yc * yc, seg) * inv_hd
    yn = yc * lax.rsqrt(var + GN_EPS) * gw_ref[...] + gb_ref[...]
    o_ref[...] = ((yn + stage["bonus"][...]) * stage["gate"][...]).astype(o_ref.dtype)


def _rwkv_kernel(*refs, lora_dims):
    n_in = 19
    (r_ref, k_ref, v_ref, zl_ref, mr_ref, mk_ref, mv_ref, ml_ref, wl_ref, w0_ref, a0_ref, kk_ref, ka_ref,
     rk_ref, gw_ref, gb_ref, seg_ref, tri_ref, zero_ref) = refs[:n_in]
    o_ref = refs[n_in]
    scratch = refs[n_in + 1:]
    state_ref, prev_ref, prevl_ref, y_ref = scratch[:4]
    handoff = scratch[4:4 + _N_HANDOFF]
    stage = dict(zip(_STAGE_NAMES, scratch[4 + _N_HANDOFF:]))

    @pl.when(pl.program_id(2) == 0)
    def _():
        state_ref[...] = jnp.zeros_like(state_ref)
        prev_ref[...] = jnp.zeros_like(prev_ref)
        prevl_ref[...] = jnp.zeros_like(prevl_ref)

    _rwkv_prepare(r_ref, k_ref, v_ref, zl_ref, mr_ref, mk_ref, mv_ref, ml_ref, wl_ref, w0_ref, a0_ref,
                  kk_ref, ka_ref, rk_ref, seg_ref, tri_ref, prev_ref, prevl_ref, stage, lora_dims)
    _rwkv_solve(stage, state_ref, y_ref, o_ref, gw_ref, gb_ref, seg_ref, handoff, zero_ref[0])


def _rwkv7(zr, zl, shift_mix, w0, w_decay_up, a0, w_aaa_up, w_gate_up, k_k, k_a, r_k, gn_w, gn_b,
           batch, seq):
    tok = zr.shape[0]
    width = zr.shape[1] // 3
    lw = zl.shape[1]
    cw = RWKV_HPB * RWKV_HEAD_DIM
    nhb = width // cw
    tblk = RWKV_TBLK
    nt = seq // tblk
    ch = RWKV_CHUNK

    def pad_rows(w, start):
        return jnp.pad(w, ((start, lw - start - w.shape[0]), (0, 0)))

    wl = jnp.stack([pad_rows(w_decay_up, 0), pad_rows(w_aaa_up, DECAY_LORA),
                    pad_rows(w_gate_up, DECAY_LORA + AAA_LORA)], axis=0)
    wl = wl.reshape(3, lw, nhb, cw).transpose(2, 1, 0, 3).reshape(nhb, lw, 3 * cw).astype(BF16)

    mix_z = shift_mix[: 3 * width].reshape(1, 3 * width)
    mix_l = shift_mix[3 * width:].reshape(1, lw)
    idx = jnp.arange(MXU_DIM) // RWKV_HEAD_DIM
    seg = (idx[:, None] == idx[None, :]).astype(BF16)
    tri = (jnp.arange(ch)[:, None] >= jnp.arange(ch)[None, :]).astype(BF16)

    def col(sec):
        return pl.BlockSpec((tblk, cw), lambda b, h, t: (b * nt + t, sec * nhb + h))

    def mixcol(sec):
        return pl.BlockSpec((1, cw), lambda b, h, t: (0, sec * nhb + h))

    vec = pl.BlockSpec((1, cw), lambda b, h, t: (0, h))
    row_vec = lambda a: a.reshape(1, width)
    stage_shapes = ([pltpu.VMEM((tblk, cw), BF16)] * len(_STAGE_BF16)
                    + [pltpu.VMEM((tblk // ch, 8, cw), F32),
                       pltpu.VMEM((tblk, cw), F32), pltpu.VMEM((tblk, cw), F32)])
    n_prob = (tblk // ch) * (cw // LANES)
    handoff_shapes = [
        pltpu.VMEM((n_prob, ch, LANES), BF16),
        pltpu.VMEM((n_prob, ch, LANES), BF16),
        pltpu.VMEM((n_prob, ch, 2 * LANES), BF16),
        pltpu.VMEM((n_prob, ch, 2 * LANES), F32),
        pltpu.VMEM((n_prob, ch, 2 * LANES), F32),
        pltpu.VMEM((n_prob, ch, LANES), BF16), pltpu.VMEM((n_prob, ch, LANES), F32),
        pltpu.VMEM((n_prob, LANES, LANES), BF16), pltpu.VMEM((n_prob, LANES, LANES), F32),
    ]
    assert len(handoff_shapes) == _N_HANDOFF
    kern = functools.partial(_rwkv_kernel, lora_dims=(DECAY_LORA, AAA_LORA, GATE_LORA))
    return pl.pallas_call(
        kern,
        grid=(batch, nhb, nt),
        in_specs=[
            col(0), col(1), col(2),
            pl.BlockSpec((tblk, lw), lambda b, h, t: (b * nt + t, 0)),
            mixcol(0), mixcol(1), mixcol(2),
            pl.BlockSpec((1, lw), lambda b, h, t: (0, 0)),
            pl.BlockSpec((None, lw, 3 * cw), lambda b, h, t: (h, 0, 0)),
            vec, vec, vec, vec, vec, vec, vec,
            pl.BlockSpec((MXU_DIM, MXU_DIM), lambda b, h, t: (0, 0)),
            pl.BlockSpec((ch, ch), lambda b, h, t: (0, 0)),
            pl.BlockSpec(memory_space=pltpu.SMEM),
        ],
        out_specs=pl.BlockSpec((tblk, cw), lambda b, h, t: (b * nt + t, h)),
        out_shape=jax.ShapeDtypeStruct((tok, width), BF16),
        scratch_shapes=[
            pltpu.VMEM((cw // LANES, LANES, LANES), F32),
            pltpu.VMEM((8, cw), F32),
            pltpu.VMEM((8, lw), F32),
            pltpu.VMEM((tblk, cw), F32),
        ] + handoff_shapes + stage_shapes,
        compiler_params=_cparams(("parallel", "parallel", "arbitrary")),
        name="rwkv7_chunked",
    )(zr, zr, zr, zl, mix_z, mix_z, mix_z, mix_l, wl,
      row_vec(w0), row_vec(a0), row_vec(k_k), row_vec(k_a), row_vec(r_k), row_vec(gn_w), row_vec(gn_b),
      seg, tri, jnp.zeros((1,), jnp.int32))


def _merge_kernel(attn_ref, rw_ref, g0_ref, g1_ref, wa_ref, wr_ref, out_ref):
    attn_d = _dot(attn_ref[...], wa_ref[...])
    rwkv_d = _dot(rw_ref[...], wr_ref[...])
    out_ref[...] = (g0_ref[...].astype(F32) * attn_d + g1_ref[...].astype(F32) * rwkv_d).astype(out_ref.dtype)


def _merge(attn, rw, gates, w_attn_up, w_rwkv_up, tm=1024, tn=512):
    tok = rw.shape[0]
    d = w_attn_up.shape[1]
    nj = d // tn
    return pl.pallas_call(
        _merge_kernel,
        grid=(tok // tm, nj),
        in_specs=[
            pl.BlockSpec((tm, attn.shape[1]), lambda i, j: (i, 0)),
            pl.BlockSpec((tm, rw.shape[1]), lambda i, j: (i, 0)),
            pl.BlockSpec((tm, tn), lambda i, j: (i, j)),
            pl.BlockSpec((tm, tn), lambda i, j: (i, nj + j)),
            pl.BlockSpec((w_attn_up.shape[0], tn), lambda i, j: (0, j)),
            pl.BlockSpec((w_rwkv_up.shape[0], tn), lambda i, j: (0, j)),
        ],
        out_specs=pl.BlockSpec((tm, tn), lambda i, j: (i, j)),
        out_shape=jax.ShapeDtypeStruct((tok, d), BF16),
        compiler_params=_cparams(("parallel", "arbitrary")),
        name="merge_gated_up",
    )(attn, rw, gates, gates, w_attn_up, w_rwkv_up)


def _ple_kernel(h_ref, wg_ref, p_ref, wp_ref, x_ref, o_ref):
    gate = jax.nn.sigmoid(_dot(h_ref[...], wg_ref[...]))
    o_ref[...] = x_ref[...] + gate * _dot(p_ref[...], wp_ref[...])


def _ple(hn, w_gate, p, w_proj, x, tm=1024, tn=512):
    m, k = hn.shape
    n = w_gate.shape[1]
    kp = p.shape[1]
    return pl.pallas_call(
        _ple_kernel,
        grid=(m // tm, n // tn),
        in_specs=[
            pl.BlockSpec((tm, k), lambda i, j: (i, 0)),
            pl.BlockSpec((k, tn), lambda i, j: (0, j)),
            pl.BlockSpec((tm, kp), lambda i, j: (i, 0)),
            pl.BlockSpec((kp, tn), lambda i, j: (0, j)),
            pl.BlockSpec((tm, tn), lambda i, j: (i, j)),
        ],
        out_specs=pl.BlockSpec((tm, tn), lambda i, j: (i, j)),
        out_shape=jax.ShapeDtypeStruct((m, n), F32),
        compiler_params=_cparams(("parallel", "arbitrary")),
        name="ple_gated",
    )(hn, w_gate, p, w_proj, x)


def _layer(x2d, p2d, batch, seq, norm_mix, w_in, q_gain, k_gain, rel_bias, w_attn_up, shift_mix, w0,
           w_decay_up, a0, w_aaa_up, w_gate_up, k_k, k_a, r_k, gn_w, gn_b, w_rwkv_up, w_out, norm_mlp,
           w_mlp_in, w_mlp_out, norm_ple, w_ple_gate, w_ple_proj):
    n_attn_heads = len(DILATED_GROUPS) * ATTN_HEADS_PER_GROUP
    attn_width = n_attn_heads * ATTN_HEAD_DIM
    rwkv_width = w_rwkv_up.shape[0]
    a_end = 3 * attn_width
    z_end = a_end + 3 * rwkv_width
    r_end = z_end + DECAY_LORA + AAA_LORA + GATE_LORA

    h = _rmsnorm_bf16(x2d, norm_mix)

    gains = jnp.stack([q_gain * (ATTN_HEAD_DIM ** -0.5), k_gain, jnp.ones_like(q_gain)]).reshape(3, 1, LANES)
    gw = ATTN_HEADS_PER_GROUP * ATTN_HEAD_DIM
    qkv_groups = []
    for gi, (_, dilation) in enumerate(DILATED_GROUPS):
        w_g = jnp.concatenate([w_in[:, sec * attn_width + gi * gw: sec * attn_width + (gi + 1) * gw]
                               for sec in range(3)], axis=1).astype(BF16)
        qkv_groups.append(_qkv_proj(h, w_g, gains, dilation, batch, seq))
    zr = _matmul(h, w_in[:, a_end:z_end].astype(BF16), name="w_in_rkv")
    zl = _matmul(h, w_in[:, z_end:r_end].astype(BF16), name="w_in_lora")
    gates = _matmul(h, w_in[:, r_end:].astype(BF16), act="sigmoid", out_dtype=BF16, name="w_in_gates")

    attn = _dilated_attention(qkv_groups, rel_bias, batch, seq)

    rw = _rwkv7(zr, zl, shift_mix, w0, w_decay_up, a0, w_aaa_up, w_gate_up, k_k, k_a, r_k, gn_w, gn_b,
                batch, seq)

    merged = _merge(attn, rw, gates, w_attn_up.astype(BF16), w_rwkv_up.astype(BF16))
    x1 = _matmul_residual(merged, w_out.astype(BF16), x2d, name="w_out_res")

    h2 = _rmsnorm_bf16(x1, norm_mlp)
    u = _matmul(h2, w_mlp_in.astype(BF16), act="relu2", out_dtype=BF16, name="mlp_in")
    x2 = _matmul_kacc_residual(u, w_mlp_out.astype(BF16), x1, name="mlp_out_res")

    hn = _rmsnorm_bf16(x2, norm_ple)
    return _ple(hn, w_ple_gate.astype(BF16), p2d.astype(BF16), w_ple_proj.astype(BF16), x2)


def kernel(x, p, norm_mix, w_in, q_gain, k_gain, rel_bias, w_attn_up, shift_mix, w0, w_decay_up, a0, w_aaa_up,
           w_gate_up, k_k, k_a, r_k, gn_w, gn_b, w_rwkv_up, w_out, norm_mlp, w_mlp_in, w_mlp_out, norm_ple,
           w_ple_gate, w_ple_proj):
    batch, seq, d_model = x.shape
    depth = p.shape[0]
    x2d = x.reshape(batch * seq, d_model)
    for i in range(depth):
        x2d = _layer(x2d, p[i].reshape(batch * seq, -1), batch, seq, norm_mix[i], w_in[i], q_gain[i], k_gain[i],
                     rel_bias, w_attn_up[i], shift_mix[i], w0[i], w_decay_up[i], a0[i], w_aaa_up[i],
                     w_gate_up[i], k_k[i], k_a[i], r_k[i], gn_w[i], gn_b[i], w_rwkv_up[i], w_out[i],
                     norm_mlp[i], w_mlp_in[i], w_mlp_out[i], norm_ple[i], w_ple_gate[i], w_ple_proj[i])
    return x2d.reshape(batch, seq, d_model)
```

```python
import functools
import math

import jax
import jax.numpy as jnp
from jax import lax
from jax.experimental import pallas as pl
from jax.experimental.pallas import tpu as pltpu

F32 = jnp.float32
BF16 = jnp.bfloat16

LANES = 128
MXU_DIM = 256
ATTN_HEAD_DIM = 128
ATTN_HEADS_PER_GROUP = 4
DILATED_GROUPS = ((128, 1), (512, 4), (2048, 16))
ATTN_BLOCK = 128
ATTN_UNITS_PER_STEP = 4
N_BUCKETS = 32
MAX_DISTANCE = 2048
RWKV_HEAD_DIM = 64
DECAY_LORA = 96
AAA_LORA = 96
GATE_LORA = 256
RMS_EPS = 1e-6
GN_EPS = 64e-5
MASK_VALUE = -1e30

RWKV_CHUNK = 64
RWKV_TBLK = 256
RWKV_HPB = 8

VMEM_LIMIT = 56 * 1024 * 1024


def _cparams(sem):
    return pltpu.CompilerParams(dimension_semantics=sem, vmem_limit_bytes=VMEM_LIMIT)


def _dot(a, b):
    return jnp.dot(a, b, preferred_element_type=F32)


def _dot_nt(a, b):
    return lax.dot_general(a, b, (((1,), (1,)), ((), ())), preferred_element_type=F32)


def _dot_split2(x, w_bf16):
    hi = x.astype(BF16)
    lo = (x - hi.astype(F32)).astype(BF16)
    return _dot(hi, w_bf16) + _dot(lo, w_bf16)


def _dot_split3_left(w_bf16, x):
    hi = x.astype(BF16)
    r1 = x - hi.astype(F32)
    mid = r1.astype(BF16)
    lo = (r1 - mid.astype(F32)).astype(BF16)
    return _dot(w_bf16, hi) + _dot(w_bf16, mid) + _dot(w_bf16, lo)


def _rmsnorm_kernel(x_ref, g_ref, o_ref):
    x = x_ref[...]
    ms = jnp.mean(x * x, axis=-1, keepdims=True)
    o_ref[...] = (x * lax.rsqrt(ms + RMS_EPS) * g_ref[...]).astype(o_ref.dtype)


def _rmsnorm_bf16(x, gain, tr=256):
    m, d = x.shape
    return pl.pallas_call(
        _rmsnorm_kernel,
        grid=(m // tr,),
        in_specs=[pl.BlockSpec((tr, d), lambda i: (i, 0)), pl.BlockSpec((1, d), lambda i: (0, 0))],
        out_specs=pl.BlockSpec((tr, d), lambda i: (i, 0)),
        out_shape=jax.ShapeDtypeStruct((m, d), BF16),
        compiler_params=_cparams(("parallel",)),
        name="rmsnorm",
    )(x, gain.reshape(1, d))


def _qkv_kernel(a_ref, w_ref, g_ref, o_ref, scr_ref, *, dilation):
    j = pl.program_id(1)
    acc = _dot(a_ref[...], w_ref[...].astype(BF16))
    rows = acc.shape[0] // dilation
    for h in range(ATTN_HEADS_PER_GROUP):
        y = acc[:, h * LANES:(h + 1) * LANES]
        ms = jnp.mean(y * y, axis=-1, keepdims=True)
        y = y * jnp.where(j < 2, lax.rsqrt(ms + RMS_EPS), 1.0) * g_ref[...]
        if dilation == 1:
            o_ref[h] = y.astype(o_ref.dtype)
        else:
            scr_ref[h] = y
            for r in range(dilation):
                o_ref[h, :, r * LANES:(r + 1) * LANES] = (
                    scr_ref[h, pl.ds(r, rows, stride=dilation), :].astype(o_ref.dtype))


def _qkv_proj(h, w_in, gains, group, dilation, batch, seq, tm=1024):
    m, d = h.shape
    tn = ATTN_HEADS_PER_GROUP * LANES
    mt_per_b = seq // tm
    hg = ATTN_HEADS_PER_GROUP
    n_groups = len(DILATED_GROUPS)
    return pl.pallas_call(
        functools.partial(_qkv_kernel, dilation=dilation),
        grid=(m // tm, 3),
        in_specs=[
            pl.BlockSpec((tm, d), lambda i, j: (i, 0)),
            pl.BlockSpec((d, tn), lambda i, j: (0, j * n_groups + group)),
            pl.BlockSpec((None, 1, LANES), lambda i, j: (j, 0, 0)),
        ],
        out_specs=pl.BlockSpec((None, hg, tm // dilation, dilation * LANES),
                               lambda i, j: (i // mt_per_b, j, i % mt_per_b, 0)),
        out_shape=jax.ShapeDtypeStruct((batch, 3 * hg, seq // dilation, dilation * LANES), BF16),
        scratch_shapes=[pltpu.VMEM((hg, tm, LANES), F32)],
        compiler_params=_cparams(("parallel", "arbitrary")),
        name=f"w_in_qkv_d{dilation}",
    )(h, w_in, gains)


def _mm_kernel(a_ref, w_ref, o_ref, *, act):
    acc = _dot(a_ref[...], w_ref[...].astype(BF16))
    if act == "sigmoid":
        acc = jax.nn.sigmoid(acc)
    elif act == "relu2":
        acc = jnp.square(jnp.maximum(acc, 0.0))
    o_ref[...] = acc.astype(o_ref.dtype)


def _matmul(a, w, *, n=None, w_col_off=0, act=None, out_dtype=F32, tm=1024, tn=512, name="matmul"):
    m, k = a.shape
    n = w.shape[1] if n is None else n
    return pl.pallas_call(
        functools.partial(_mm_kernel, act=act),
        grid=(m // tm, n // tn),
        in_specs=[pl.BlockSpec((tm, k), lambda i, j: (i, 0)),
                  pl.BlockSpec((k, tn), lambda i, j: (0, w_col_off + j))],
        out_specs=pl.BlockSpec((tm, tn), lambda i, j: (i, j)),
        out_shape=jax.ShapeDtypeStruct((m, n), out_dtype),
        compiler_params=_cparams(("parallel", "arbitrary")),
        name=name,
    )(a, w)


def _mm_res_kernel(a_ref, w_ref, r_ref, o_ref):
    o_ref[...] = r_ref[...] + _dot(a_ref[...], w_ref[...].astype(BF16))


def _matmul_residual(a, w, res, *, tm=1024, tn=512, name="matmul_res"):
    m, k = a.shape
    n = w.shape[1]
    return pl.pallas_call(
        _mm_res_kernel,
        grid=(m // tm, n // tn),
        in_specs=[
            pl.BlockSpec((tm, k), lambda i, j: (i, 0)),
            pl.BlockSpec((k, tn), lambda i, j: (0, j)),
            pl.BlockSpec((tm, tn), lambda i, j: (i, j)),
        ],
        out_specs=pl.BlockSpec((tm, tn), lambda i, j: (i, j)),
        out_shape=jax.ShapeDtypeStruct((m, n), F32),
        compiler_params=_cparams(("parallel", "arbitrary")),
        name=name,
    )(a, w, res)


def _mm_kacc_res_kernel(a_ref, w_ref, r_ref, o_ref):
    k = pl.program_id(2)

    @pl.when(k == 0)
    def _():
        o_ref[...] = r_ref[...] + _dot(a_ref[...], w_ref[...].astype(BF16))

    @pl.when(k > 0)
    def _():
        o_ref[...] += _dot(a_ref[...], w_ref[...].astype(BF16))


def _matmul_kacc_residual(a, w, res, *, tm=1024, tn=1024, tk=2048, name="matmul_kacc"):
    m, kdim = a.shape
    n = w.shape[1]
    return pl.pallas_call(
        _mm_kacc_res_kernel,
        grid=(m // tm, n // tn, kdim // tk),
        in_specs=[
            pl.BlockSpec((tm, tk), lambda i, j, k: (i, k)),
            pl.BlockSpec((tk, tn), lambda i, j, k: (k, j)),
            pl.BlockSpec((tm, tn), lambda i, j, k: (i, j)),
        ],
        out_specs=pl.BlockSpec((tm, tn), lambda i, j, k: (i, j)),
        out_shape=jax.ShapeDtypeStruct((m, n), F32),
        compiler_params=_cparams(("parallel", "parallel", "arbitrary")),
        name=name,
    )(a, w, res)


def _t5_bucket(dist):
    max_exact = N_BUCKETS // 2
    d_f = jnp.maximum(dist, 1).astype(F32)
    large = max_exact + (jnp.log(d_f / max_exact) / math.log(MAX_DISTANCE / max_exact)
                         * (N_BUCKETS - max_exact)).astype(jnp.int32)
    large = jnp.minimum(large, N_BUCKETS - 1)
    return jnp.where(dist < max_exact, dist, large)


def _band_buckets():
    blk = ATTN_BLOCK
    rel = (blk + jnp.arange(blk))[:, None] - jnp.arange(2 * blk)[None, :]
    out = []
    for window, dilation in DILATED_GROUPS:
        band = (rel >= 0) & (rel <= window // dilation)
        out.append(jnp.where(band, _t5_bucket(jnp.maximum(rel, 0) * dilation), -1))
    return jnp.stack(out).astype(jnp.int32)


def _attn_kernel(q0, k0, v0, q1, k1, v1, q2, k2, v2, bucket_ref, tab_ref, o_ref, oacc_ref, lacc_ref):
    blk = ATTN_BLOCK
    hh = pl.program_id(1)
    seq = oacc_ref.shape[0]
    refs = ((q0, k0, v0), (q1, k1, v1), (q2, k2, v2))

    def softmax_out(s, v):
        m = jnp.max(s, axis=-1, keepdims=True)
        e = jnp.exp(s - m)
        l = jnp.sum(e, axis=-1, keepdims=True)
        o = _dot(e.astype(BF16), v) / l
        return o, jnp.broadcast_to(m + jnp.log(l), (blk, LANES))

    for g, (_, dilation) in enumerate(DILATED_GROUPS):
        q_ref, k_ref, v_ref = refs[g]
        n_blocks = seq // (dilation * blk)
        bucket = bucket_ref[g]
        head = g * ATTN_HEADS_PER_GROUP + hh
        bias = jnp.where(bucket < 0, MASK_VALUE, 0.0)
        for b in range(N_BUCKETS):
            bias = jnp.where(bucket == b, tab_ref[b, head], bias)

        def merge_store(tok_rows, o, lse, g=g):
            if g == 0:
                oacc_ref[tok_rows, :] = o
                lacc_ref[tok_rows, :] = lse
            else:
                o_old = oacc_ref[tok_rows, :]
                l_old = lacc_ref[tok_rows, :]
                m = jnp.maximum(l_old, lse)
                w_old = jnp.exp(l_old - m)
                w_new = jnp.exp(lse - m)
                den = w_old + w_new
                oacc_ref[tok_rows, :] = (w_old * o_old + w_new * o) / den
                lacc_ref[tok_rows, :] = m + jnp.log(den)

        def tok_rows(off, r, dilation=dilation):
            if dilation == 1:
                return pl.ds(off, blk)
            return pl.ds(off * dilation + r, blk, stride=dilation)

        def unit(off, r, q_ref=q_ref, k_ref=k_ref, v_ref=v_ref, bias=bias, merge_store=merge_store,
                 tok_rows=tok_rows):
            cs = slice(r * LANES, (r + 1) * LANES)
            s = _dot_nt(q_ref[pl.ds(off, blk), cs], k_ref[pl.ds(off - blk, 2 * blk), cs]) + bias
            o, lse = softmax_out(s, v_ref[pl.ds(off - blk, 2 * blk), cs])
            merge_store(tok_rows(off, r), o, lse)

        for r in range(dilation):
            cs = slice(r * LANES, (r + 1) * LANES)
            s0 = _dot_nt(q_ref[0:blk, cs], k_ref[0:blk, cs]) + bias[:, blk:]
            o0, lse0 = softmax_out(s0, v_ref[0:blk, cs])
            merge_store(tok_rows(0, r), o0, lse0)

        ways = ATTN_UNITS_PER_STEP
        if dilation == 1:
            part = n_blocks // ways
            for w in range(1, ways):
                unit(w * part * blk, 0)

            def body(n, carry, unit=unit, part=part):
                off = pl.multiple_of(n * blk, blk)
                for w in range(ways):
                    unit(off + w * part * blk, 0)
                return carry

            lax.fori_loop(1, part, body, 0)
        else:
            for r in range(0, dilation, ways):
                def body(n, carry, unit=unit, r=r):
                    off = pl.multiple_of(n * blk, blk)
                    for w in range(ways):
                        unit(off, r + w)
                    return carry

                lax.fori_loop(1, n_blocks, body, 0)

    o_ref[...] = oacc_ref[...].astype(o_ref.dtype)


def _dilated_attention(qkv_groups, rel_bias, batch, seq):
    hg = ATTN_HEADS_PER_GROUP
    in_specs, operands = [], []
    for (_, dilation), arr in zip(DILATED_GROUPS, qkv_groups):
        for section in range(3):
            in_specs.append(pl.BlockSpec((None, None, seq // dilation, dilation * LANES),
                                         lambda b, h, section=section: (b, section * hg + h, 0, 0)))
            operands.append(arr)
    buckets = _band_buckets()
    in_specs.append(pl.BlockSpec(buckets.shape, lambda b, h: (0, 0, 0)))
    in_specs.append(pl.BlockSpec(memory_space=pltpu.SMEM))
    return pl.pallas_call(
        _attn_kernel,
        grid=(batch, hg),
        in_specs=in_specs,
        out_specs=pl.BlockSpec((seq, LANES), lambda b, h: (b, h)),
        out_shape=jax.ShapeDtypeStruct((batch * seq, hg * LANES), BF16),
        scratch_shapes=[pltpu.VMEM((seq, LANES), F32), pltpu.VMEM((seq, LANES), F32)],
        compiler_params=_cparams(("parallel", "parallel")),
        name="dilated_attn",
    )(*operands, buckets, rel_bias.astype(F32))


_STAGE_BF16 = ("at", "rt", "bt", "kt", "bh", "kh", "vb")
_STAGE_F32 = ("egc", "bonus", "gate")
_STAGE_NAMES = _STAGE_BF16 + _STAGE_F32
_N_HANDOFF = 9


def _seg_sum(x, seg):
    w = seg.shape[0]
    return jnp.concatenate([_dot_split2(x[:, i * w:(i + 1) * w], seg) for i in range(x.shape[1] // w)], axis=1)


def _rwkv_prepare(r_ref, k_ref, v_ref, zl_ref, mr_ref, mk_ref, mv_ref, ml_ref, wl_ref, w0_ref, a0_ref, kk_ref,
                  ka_ref, rk_ref, seg_ref, tri_ref, prev_ref, prevl_ref, stage, lora_dims):
    tblk, cw = r_ref.shape
    ch = RWKV_CHUNK
    row = lax.broadcasted_iota(jnp.int32, (tblk, 1), 0)

    def shift(z, prev_row, mix):
        prev = jnp.where(row == 0, prev_row, pltpu.roll(z, 1, axis=0))
        return z + mix * (prev - z)

    zr, zk, zv, zl = r_ref[...], k_ref[...], v_ref[...], zl_ref[...]
    r = shift(zr, prev_ref[0:1, :], mr_ref[...])
    k = shift(zk, prev_ref[1:2, :], mk_ref[...])
    v = shift(zv, prev_ref[2:3, :], mv_ref[...])
    xl = shift(zl, prevl_ref[0:1, :], ml_ref[...])
    prev_ref[0:1, :] = zr[tblk - 1:tblk, :]
    prev_ref[1:2, :] = zk[tblk - 1:tblk, :]
    prev_ref[2:3, :] = zv[tblk - 1:tblk, :]
    prevl_ref[0:1, :] = zl[tblk - 1:tblk, :]

    d_w, d_a, _ = lora_dims
    lane_l = lax.broadcasted_iota(jnp.int32, xl.shape, 1)
    act = jnp.where(lane_l < d_w, jnp.tanh(xl), jnp.where(lane_l < d_w + d_a, xl, jax.nn.sigmoid(xl)))
    up = _dot(act.astype(BF16), wl_ref[...])
    w_arg = -(w0_ref[...] + up[:, 0:cw])
    softplus = jnp.maximum(w_arg, 0.0) + jnp.log1p(jnp.exp(-jnp.abs(w_arg)))
    logdecay = -jnp.exp(-softplus - 0.5)
    a_gate = jax.nn.sigmoid(a0_ref[...] + up[:, cw:2 * cw])
    stage["gate"][...] = up[:, 2 * cw:3 * cw]

    seg = seg_ref[...]
    kk = k * kk_ref[...]
    kk = kk * lax.rsqrt(jnp.maximum(_seg_sum(kk * kk, seg), 1e-24))
    k2 = k * (1.0 + (a_gate - 1.0) * ka_ref[...])
    a_vec = -kk
    b_vec = kk * a_gate
    stage["bonus"][...] = _seg_sum(r * k2 * rk_ref[...], seg) * v

    tri = tri_ref[...]
    for c in range(tblk // ch):
        rows = slice(c * ch, (c + 1) * ch)
        ld = logdecay[rows]
        g_inc = _dot_split3_left(tri, ld)
        g_exc = g_inc - ld
        e_neg = jnp.exp(-g_inc)
        e_last = jnp.exp(g_inc[ch - 1:ch, :])
        bt = b_vec[rows] * e_neg
        kt = k2[rows] * e_neg
        stage["at"][rows, :] = (a_vec[rows] * jnp.exp(g_exc)).astype(BF16)
        stage["rt"][rows, :] = (r[rows] * jnp.exp(g_inc)).astype(BF16)
        stage["bt"][rows, :] = bt.astype(BF16)
        stage["kt"][rows, :] = kt.astype(BF16)
        stage["bh"][rows, :] = (bt * e_last).astype(BF16)
        stage["kh"][rows, :] = (kt * e_last).astype(BF16)
        stage["egc"][c] = jnp.broadcast_to(e_last, stage["egc"].shape[1:])
    stage["vb"][...] = v.astype(BF16)


def _rwkv_solve(stage, state_ref, y_ref, o_ref, gw_ref, gb_ref, seg_ref, handoff, zero_off):
    tblk, cw = o_ref.shape
    ch = RWKV_CHUNK
    n_chunks = tblk // ch
    n_pairs = cw // LANES
    hd = RWKV_HEAD_DIM

    lane = lax.broadcasted_iota(jnp.int32, (1, LANES), 1)
    m0, m1 = lane < hd, lane >= hd
    lane2 = lax.broadcasted_iota(jnp.int32, (1, 2 * LANES), 1) % LANES
    m0w, m1w = lane2 < hd, lane2 >= hd
    rr = lax.broadcasted_iota(jnp.int32, (ch, 2 * LANES), 0)
    cc = lax.broadcasted_iota(jnp.int32, (ch, 2 * LANES), 1) % ch
    strict_lower = cc < rr
    incl_lower = cc <= rr
    eye = (lax.broadcasted_iota(jnp.int32, (LANES, LANES), 0)
           == lax.broadcasted_iota(jnp.int32, (LANES, LANES), 1)).astype(F32)
    bd_r = lax.broadcasted_iota(jnp.int32, (LANES, LANES), 0) >= hd
    bd_c = lax.broadcasted_iota(jnp.int32, (LANES, LANES), 1) >= hd
    block_diag = bd_r == bd_c
    zeros_cv = jnp.zeros((ch, LANES), BF16)

    def keep(mask, x):
        return jnp.where(mask, x, jnp.zeros_like(x))

    problems = [(c, p) for c in range(n_chunks) for p in range(n_pairs)]

    def blk(name, cp):
        c, p = cp
        return stage[name][c * ch:(c + 1) * ch, p * LANES:(p + 1) * LANES]

    aakv_s, tcat_s, g2_s, wu_s, ry_s, rt2_s, y0_s, mlr_s, n0_s = handoff

    g1, pw, tinv = {}, {}, {}
    for i, cp in enumerate(problems):
        bt, kt = blk("bt", cp), blk("kt", cp)
        x = jnp.concatenate([blk("at", cp), blk("rt", cp)], axis=0)
        ym = jnp.concatenate([keep(m0, bt), keep(m0, kt), keep(m1, kt), keep(m1, bt)], axis=0)
        g_all = _dot_nt(x, ym)
        g1[cp] = jnp.where(strict_lower, g_all[:ch], 0.0)
        g2_s[i] = jnp.where(incl_lower, g_all[ch:], 0.0).astype(BF16)
        pw[cp] = jnp.concatenate([keep(m0, g1[cp][:, :LANES]), keep(m1, g1[cp][:, LANES:])], axis=0)
        tinv[cp] = eye + pw[cp]
    for _ in range(int(math.log2(ch)) - 1):
        for cp in problems:
            pb = pw[cp].astype(BF16)
            pw[cp] = _dot(pb, pb)
        for cp in problems:
            tinv[cp] = tinv[cp] + _dot(tinv[cp].astype(BF16), pw[cp].astype(BF16))
    for i, cp in enumerate(problems):
        vb = blk("vb", cp)
        v_stack = jnp.concatenate([zeros_cv, keep(m0, vb), keep(m1, vb), zeros_cv], axis=0)
        aakv_s[i] = _dot(g1[cp].astype(BF16), v_stack).astype(BF16)
        tcat_s[i] = (tinv[cp][:ch] + tinv[cp][ch:]).astype(BF16)
    for i, cp in enumerate(problems):
        j = zero_off + i
        rhs = jnp.concatenate([blk("at", cp), aakv_s[j]], axis=1)
        wu_s[i] = _dot(tcat_s[j], jnp.concatenate([keep(m0w, rhs), keep(m1w, rhs)], axis=0))
    for i, cp in enumerate(problems):
        j = zero_off + i
        upper = wu_s[j].astype(BF16)
        lower = jnp.concatenate([zeros_cv, blk("vb", cp)], axis=1)
        rhs = jnp.concatenate([keep(m0w, upper), keep(m0w, lower), keep(m1w, lower), keep(m1w, upper)], axis=0)
        ry_s[i] = _dot(g2_s[j], rhs)
    for i, cp in enumerate(problems):
        j = zero_off + i
        wu, ry = wu_s[j], ry_s[j]
        bkh = jnp.concatenate([blk("bh", cp), blk("kh", cp)], axis=0)
        rt2_s[i] = (blk("rt", cp).astype(F32) + ry[:, :LANES]).astype(BF16)
        y0_s[i] = ry[:, LANES:]
        wt_t = wu[:, :LANES].T.astype(BF16)
        mlr_s[i] = jnp.where(block_diag, _dot(wt_t, bkh[:ch]), 0.0).astype(BF16)
        uv_t = jnp.concatenate([wu[:, LANES:], blk("vb", cp).astype(F32)], axis=0).T.astype(BF16)
        n0_s[i] = jnp.where(block_diag, _dot(uv_t, bkh), 0.0)

    for c in range(n_chunks):
        rows = slice(c * ch, (c + 1) * ch)
        for p in range(n_pairs):
            cs = slice(p * LANES, (p + 1) * LANES)
            j = zero_off + (c * n_pairs + p)
            ht = state_ref[p]
            htb = ht.astype(BF16)
            y_ref[rows, cs] = _dot_nt(rt2_s[j], htb) + y0_s[j]
            state_ref[p] = ht * stage["egc"][c, 0:1, cs] + _dot(htb, mlr_s[j]) + n0_s[j]

    seg = seg_ref[...]
    y = y_ref[...]
    inv_hd = 1.0 / hd
    mu = _seg_sum(y, seg) * inv_hd
    yc = y - mu
    var = _seg_sum(yc * yc, seg) * inv_hd
    yn = yc * lax.rsqrt(var + GN_EPS) * gw_ref[...] + gb_ref[...]
    o_ref[...] = ((yn + stage["bonus"][...]) * stage["gate"][...]).astype(o_ref.dtype)


def _rwkv_kernel(*refs, lora_dims):
    n_in = 19
    (r_ref, k_ref, v_ref, zl_ref, mr_ref, mk_ref, mv_ref, ml_ref, wl_ref, w0_ref, a0_ref, kk_ref, ka_ref,
     rk_ref, gw_ref, gb_ref, seg_ref, tri_ref, zero_ref) = refs[:n_in]
    o_ref = refs[n_in]
    scratch = refs[n_in + 1:]
    state_ref, prev_ref, prevl_ref, y_ref = scratch[:4]
    handoff = scratch[4:4 + _N_HANDOFF]
    stage = dict(zip(_STAGE_NAMES, scratch[4 + _N_HANDOFF:]))

    @pl.when(pl.program_id(2) == 0)
    def _():
        state_ref[...] = jnp.zeros_like(state_ref)
        prev_ref[...] = jnp.zeros_like(prev_ref)
        prevl_ref[...] = jnp.zeros_like(prevl_ref)

    _rwkv_prepare(r_ref, k_ref, v_ref, zl_ref, mr_ref, mk_ref, mv_ref, ml_ref, wl_ref, w0_ref, a0_ref,
                  kk_ref, ka_ref, rk_ref, seg_ref, tri_ref, prev_ref, prevl_ref, stage, lora_dims)
    _rwkv_solve(stage, state_ref, y_ref, o_ref, gw_ref, gb_ref, seg_ref, handoff, zero_ref[0])


def _rwkv7(zr, zl, shift_mix, w0, w_decay_up, a0, w_aaa_up, w_gate_up, k_k, k_a, r_k, gn_w, gn_b,
           batch, seq):
    tok = zr.shape[0]
    width = zr.shape[1] // 3
    lw = zl.shape[1]
    cw = RWKV_HPB * RWKV_HEAD_DIM
    nhb = width // cw
    tblk = RWKV_TBLK
    nt = seq // tblk
    ch = RWKV_CHUNK

    def pad_rows(w, start):
        return jnp.pad(w, ((start, lw - start - w.shape[0]), (0, 0)))

    wl = jnp.stack([pad_rows(w_decay_up, 0), pad_rows(w_aaa_up, DECAY_LORA),
                    pad_rows(w_gate_up, DECAY_LORA + AAA_LORA)], axis=0)
    wl = wl.reshape(3, lw, nhb, cw).transpose(2, 1, 0, 3).reshape(nhb, lw, 3 * cw).astype(BF16)

    mix_z = shift_mix[: 3 * width].reshape(1, 3 * width)
    mix_l = shift_mix[3 * width:]
    mix_l = jnp.pad(mix_l, (0, lw - mix_l.shape[0])).reshape(1, lw)
    idx = jnp.arange(MXU_DIM) // RWKV_HEAD_DIM
    seg = (idx[:, None] == idx[None, :]).astype(BF16)
    tri = (jnp.arange(ch)[:, None] >= jnp.arange(ch)[None, :]).astype(BF16)

    def col(sec):
        return pl.BlockSpec((tblk, cw), lambda b, h, t: (b * nt + t, sec * nhb + h))

    def mixcol(sec):
        return pl.BlockSpec((1, cw), lambda b, h, t: (0, sec * nhb + h))

    vec = pl.BlockSpec((1, cw), lambda b, h, t: (0, h))
    row_vec = lambda a: a.reshape(1, width)
    stage_shapes = ([pltpu.VMEM((tblk, cw), BF16)] * len(_STAGE_BF16)
                    + [pltpu.VMEM((tblk // ch, 8, cw), F32),
                       pltpu.VMEM((tblk, cw), F32), pltpu.VMEM((tblk, cw), F32)])
    n_prob = (tblk // ch) * (cw // LANES)
    handoff_shapes = [
        pltpu.VMEM((n_prob, ch, LANES), BF16),
        pltpu.VMEM((n_prob, ch, LANES), BF16),
        pltpu.VMEM((n_prob, ch, 2 * LANES), BF16),
        pltpu.VMEM((n_prob, ch, 2 * LANES), F32),
        pltpu.VMEM((n_prob, ch, 2 * LANES), F32),
        pltpu.VMEM((n_prob, ch, LANES), BF16), pltpu.VMEM((n_prob, ch, LANES), F32),
        pltpu.VMEM((n_prob, LANES, LANES), BF16), pltpu.VMEM((n_prob, LANES, LANES), F32),
    ]
    assert len(handoff_shapes) == _N_HANDOFF
    kern = functools.partial(_rwkv_kernel, lora_dims=(DECAY_LORA, AAA_LORA, GATE_LORA))
    return pl.pallas_call(
        kern,
        grid=(batch, nhb, nt),
        in_specs=[
            col(0), col(1), col(2),
            pl.BlockSpec((tblk, lw), lambda b, h, t: (b * nt + t, 0)),
            mixcol(0), mixcol(1), mixcol(2),
            pl.BlockSpec((1, lw), lambda b, h, t: (0, 0)),
            pl.BlockSpec((None, lw, 3 * cw), lambda b, h, t: (h, 0, 0)),
            vec, vec, vec, vec, vec, vec, vec,
            pl.BlockSpec((MXU_DIM, MXU_DIM), lambda b, h, t: (0, 0)),
            pl.BlockSpec((ch, ch), lambda b, h, t: (0, 0)),
            pl.BlockSpec(memory_space=pltpu.SMEM),
        ],
        out_specs=pl.BlockSpec((tblk, cw), lambda b, h, t: (b * nt + t, h)),
        out_shape=jax.ShapeDtypeStruct((tok, width), BF16),
        scratch_shapes=[
            pltpu.VMEM((cw // LANES, LANES, LANES), F32),
            pltpu.VMEM((8, cw), F32),
            pltpu.VMEM((8, lw), F32),
            pltpu.VMEM((tblk, cw), F32),
        ] + handoff_shapes + stage_shapes,
        compiler_params=_cparams(("parallel", "parallel", "arbitrary")),
        name="rwkv7_chunked",
    )(zr, zr, zr, zl, mix_z, mix_z, mix_z, mix_l, wl,
      row_vec(w0), row_vec(a0), row_vec(k_k), row_vec(k_a), row_vec(r_k), row_vec(gn_w), row_vec(gn_b),
      seg, tri, jnp.zeros((1,), jnp.int32))


def _merge_kernel(attn_ref, rw_ref, g0_ref, g1_ref, wa_ref, wr_ref, out_ref):
    attn_d = _dot(attn_ref[...], wa_ref[...].astype(BF16))
    rwkv_d = _dot(rw_ref[...], wr_ref[...].astype(BF16))
    out_ref[...] = (g0_ref[...].astype(F32) * attn_d + g1_ref[...].astype(F32) * rwkv_d).astype(out_ref.dtype)


def _merge(attn, rw, gates, w_attn_up, w_rwkv_up, tm=1024, tn=512):
    tok = rw.shape[0]
    d = w_attn_up.shape[1]
    nj = d // tn
    return pl.pallas_call(
        _merge_kernel,
        grid=(tok // tm, nj),
        in_specs=[
            pl.BlockSpec((tm, attn.shape[1]), lambda i, j: (i, 0)),
            pl.BlockSpec((tm, rw.shape[1]), lambda i, j: (i, 0)),
            pl.BlockSpec((tm, tn), lambda i, j: (i, j)),
            pl.BlockSpec((tm, tn), lambda i, j: (i, nj + j)),
            pl.BlockSpec((w_attn_up.shape[0], tn), lambda i, j: (0, j)),
            pl.BlockSpec((w_rwkv_up.shape[0], tn), lambda i, j: (0, j)),
        ],
        out_specs=pl.BlockSpec((tm, tn), lambda i, j: (i, j)),
        out_shape=jax.ShapeDtypeStruct((tok, d), BF16),
        compiler_params=_cparams(("parallel", "arbitrary")),
        name="merge_gated_up",
    )(attn, rw, gates, gates, w_attn_up, w_rwkv_up)


def _ple_kernel(h_ref, wg_ref, p_ref, wp_ref, x_ref, o_ref):
    gate = jax.nn.sigmoid(_dot(h_ref[...], wg_ref[...].astype(BF16)))
    o_ref[...] = x_ref[...] + gate * _dot(p_ref[...], wp_ref[...].astype(BF16))


def _ple(hn, w_gate, p, w_proj, x, tm=1024, tn=512):
    m, k = hn.shape
    n = w_gate.shape[1]
    kp = p.shape[1]
    return pl.pallas_call(
        _ple_kernel,
        grid=(m // tm, n // tn),
        in_specs=[
            pl.BlockSpec((tm, k), lambda i, j: (i, 0)),
            pl.BlockSpec((k, tn), lambda i, j: (0, j)),
            pl.BlockSpec((tm, kp), lambda i, j: (i, 0)),
            pl.BlockSpec((kp, tn), lambda i, j: (0, j)),
            pl.BlockSpec((tm, tn), lambda i, j: (i, j)),
        ],
        out_specs=pl.BlockSpec((tm, tn), lambda i, j: (i, j)),
        out_shape=jax.ShapeDtypeStruct((m, n), F32),
        compiler_params=_cparams(("parallel", "arbitrary")),
        name="ple_gated",
    )(hn, w_gate, p, w_proj, x)


def _layer(x2d, p2d, batch, seq, norm_mix, w_in, q_gain, k_gain, rel_bias, w_attn_up, shift_mix, w0,
           w_decay_up, a0, w_aaa_up, w_gate_up, k_k, k_a, r_k, gn_w, gn_b, w_rwkv_up, w_out, norm_mlp,
           w_mlp_in, w_mlp_out, norm_ple, w_ple_gate, w_ple_proj):
    n_attn_heads = len(DILATED_GROUPS) * ATTN_HEADS_PER_GROUP
    attn_width = n_attn_heads * ATTN_HEAD_DIM
    rwkv_width = w_rwkv_up.shape[0]
    a_end = 3 * attn_width
    z_end = a_end + 3 * rwkv_width
    r_end = z_end + DECAY_LORA + AAA_LORA + GATE_LORA

    h = _rmsnorm_bf16(x2d, norm_mix)

    gains = jnp.stack([q_gain * (ATTN_HEAD_DIM ** -0.5), k_gain, jnp.ones_like(q_gain)]).reshape(3, 1, LANES)
    tn = ATTN_HEADS_PER_GROUP * ATTN_HEAD_DIM
    qkv_groups = [_qkv_proj(h, w_in, gains, gi, dilation, batch, seq)
                  for gi, (_, dilation) in enumerate(DILATED_GROUPS)]
    zr = _matmul(h, w_in, n=z_end - a_end, w_col_off=a_end // tn, tn=tn, name="w_in_rkv")
    zl = _matmul(h, w_in, n=tn, w_col_off=z_end // tn, tn=tn, name="w_in_lora")
    gates = _matmul(h, w_in[:, r_end:].astype(BF16), act="sigmoid", out_dtype=BF16, tn=1024, name="w_in_gates")

    attn = _dilated_attention(qkv_groups, rel_bias, batch, seq)

    rw = _rwkv7(zr, zl, shift_mix, w0, w_decay_up, a0, w_aaa_up, w_gate_up, k_k, k_a, r_k, gn_w, gn_b,
                batch, seq)

    merged = _merge(attn, rw, gates, w_attn_up, w_rwkv_up)
    x1 = _matmul_residual(merged, w_out, x2d, name="w_out_res")

    h2 = _rmsnorm_bf16(x1, norm_mlp)
    u = _matmul(h2, w_mlp_in, act="relu2", out_dtype=BF16, name="mlp_in")
    x2 = _matmul_kacc_residual(u, w_mlp_out, x1, name="mlp_out_res")

    hn = _rmsnorm_bf16(x2, norm_ple)
    return _ple(hn, w_ple_gate, p2d.astype(BF16), w_ple_proj, x2)


def kernel(x, p, norm_mix, w_in, q_gain, k_gain, rel_bias, w_attn_up, shift_mix, w0, w_decay_up, a0, w_aaa_up,
           w_gate_up, k_k, k_a, r_k, gn_w, gn_b, w_rwkv_up, w_out, norm_mlp, w_mlp_in, w_mlp_out, norm_ple,
           w_ple_gate, w_ple_proj):
    batch, seq, d_model = x.shape
    depth = p.shape[0]
    x2d = x.reshape(batch * seq, d_model)
    for i in range(depth):
        x2d = _layer(x2d, p[i].reshape(batch * seq, -1), batch, seq, norm_mix[i], w_in[i], q_gain[i], k_gain[i],
                     rel_bias, w_attn_up[i], shift_mix[i], w0[i], w_decay_up[i], a0[i], w_aaa_up[i],
                     w_gate_up[i], k_k[i], k_a[i], r_k[i], gn_w[i], gn_b[i], w_rwkv_up[i], w_out[i],
                     norm_mlp[i], w_mlp_in[i], w_mlp_out[i], norm_ple[i], w_ple_gate[i], w_ple_proj[i])
    return x2d.reshape(batch, seq, d_model)
```

```python
import functools
import math

import jax
import jax.numpy as jnp
from jax import lax
from jax.experimental import pallas as pl
from jax.experimental.pallas import tpu as pltpu

F32 = jnp.float32
BF16 = jnp.bfloat16

LANES = 128
SUBLANES = 8
MXU_DIM = 256
ATTN_HEAD_DIM = 128
ATTN_HEADS_PER_GROUP = 4
DILATED_GROUPS = ((128, 1), (512, 4), (2048, 16))
ATTN_BLOCK = 128
ATTN_UNITS_PER_STEP = 4
N_BUCKETS = 32
MAX_DISTANCE = 2048
RWKV_HEAD_DIM = 64
DECAY_LORA = 96
AAA_LORA = 96
GATE_LORA = 256
RMS_EPS = 1e-6
GN_EPS = 64e-5
MASK_VALUE = -1e30

RWKV_CHUNK = 64
RWKV_TBLK = 256
RWKV_HPB = 8

VMEM_LIMIT = 56 * 1024 * 1024


def _cparams(sem):
    return pltpu.CompilerParams(dimension_semantics=sem, vmem_limit_bytes=VMEM_LIMIT)


def _dot(a, b):
    return jnp.dot(a, b, preferred_element_type=F32)


def _dot_nt(a, b):
    return lax.dot_general(a, b, (((1,), (1,)), ((), ())), preferred_element_type=F32)


def _dot_split2(x, w_bf16):
    hi = x.astype(BF16)
    lo = (x - hi.astype(F32)).astype(BF16)
    return _dot(hi, w_bf16) + _dot(lo, w_bf16)


def _dot_split3_left(w_bf16, x):
    hi = x.astype(BF16)
    r1 = x - hi.astype(F32)
    mid = r1.astype(BF16)
    lo = (r1 - mid.astype(F32)).astype(BF16)
    return _dot(w_bf16, hi) + _dot(w_bf16, mid) + _dot(w_bf16, lo)


def _rmsnorm_kernel(x_ref, g_ref, o_ref):
    x = x_ref[...]
    ms = jnp.mean(x * x, axis=-1, keepdims=True)
    o_ref[...] = (x * lax.rsqrt(ms + RMS_EPS) * g_ref[...]).astype(o_ref.dtype)


def _rmsnorm_bf16(x, gain, tr=256):
    m, d = x.shape
    return pl.pallas_call(
        _rmsnorm_kernel,
        grid=(m // tr,),
        in_specs=[pl.BlockSpec((tr, d), lambda i: (i, 0)), pl.BlockSpec((1, d), lambda i: (0, 0))],
        out_specs=pl.BlockSpec((tr, d), lambda i: (i, 0)),
        out_shape=jax.ShapeDtypeStruct((m, d), BF16),
        compiler_params=_cparams(("parallel",)),
        name="rmsnorm",
    )(x, gain.reshape(1, d))


def _qkv_kernel(a_ref, w_ref, g_ref, o_ref, scr_ref, *, dilation):
    j = pl.program_id(1)
    acc = _dot(a_ref[...], w_ref[...].astype(BF16))
    rows = acc.shape[0] // dilation
    for h in range(ATTN_HEADS_PER_GROUP):
        y = acc[:, h * LANES:(h + 1) * LANES]
        ms = jnp.mean(y * y, axis=-1, keepdims=True)
        y = y * jnp.where(j < 2, lax.rsqrt(ms + RMS_EPS), 1.0) * g_ref[...]
        if dilation == 1:
            o_ref[h] = y.astype(o_ref.dtype)
        else:
            scr_ref[h] = y
            for r in range(dilation):
                o_ref[h, :, r * LANES:(r + 1) * LANES] = (
                    scr_ref[h, pl.ds(r, rows, stride=dilation), :].astype(o_ref.dtype))


def _qkv_proj(h, w_in, gains, group, dilation, batch, seq, tm=1024):
    m, d = h.shape
    tn = ATTN_HEADS_PER_GROUP * LANES
    mt_per_b = seq // tm
    hg = ATTN_HEADS_PER_GROUP
    n_groups = len(DILATED_GROUPS)
    return pl.pallas_call(
        functools.partial(_qkv_kernel, dilation=dilation),
        grid=(m // tm, 3),
        in_specs=[
            pl.BlockSpec((tm, d), lambda i, j: (i, 0)),
            pl.BlockSpec((d, tn), lambda i, j: (0, j * n_groups + group)),
            pl.BlockSpec((None, 1, LANES), lambda i, j: (j, 0, 0)),
        ],
        out_specs=pl.BlockSpec((None, hg, tm // dilation, dilation * LANES),
                               lambda i, j: (i // mt_per_b, j, i % mt_per_b, 0)),
        out_shape=jax.ShapeDtypeStruct((batch, 3 * hg, seq // dilation, dilation * LANES), BF16),
        scratch_shapes=[pltpu.VMEM((hg, tm, LANES), F32)],
        compiler_params=_cparams(("parallel", "arbitrary")),
        name=f"w_in_qkv_d{dilation}",
    )(h, w_in, gains)


def _mm_kernel(a_ref, w_ref, o_ref, *, act):
    acc = _dot(a_ref[...], w_ref[...].astype(BF16))
    if act == "sigmoid":
        acc = jax.nn.sigmoid(acc)
    elif act == "relu2":
        acc = jnp.square(jnp.maximum(acc, 0.0))
    o_ref[...] = acc.astype(o_ref.dtype)


def _matmul(a, w, *, n=None, w_col_off=0, act=None, out_dtype=F32, tm=1024, tn=512, name="matmul"):
    m, k = a.shape
    n = w.shape[1] if n is None else n
    return pl.pallas_call(
        functools.partial(_mm_kernel, act=act),
        grid=(m // tm, n // tn),
        in_specs=[pl.BlockSpec((tm, k), lambda i, j: (i, 0)),
                  pl.BlockSpec((k, tn), lambda i, j: (0, w_col_off + j))],
        out_specs=pl.BlockSpec((tm, tn), lambda i, j: (i, j)),
        out_shape=jax.ShapeDtypeStruct((m, n), out_dtype),
        compiler_params=_cparams(("parallel", "arbitrary")),
        name=name,
    )(a, w)


def _mm_shift_kernel(a_ref, w_ref, mix_ref, o_ref, carry_ref, *, m_tiles_per_seq):
    i = pl.program_id(0)
    j = pl.program_id(1)

    @pl.when(i == 0)
    def _():
        carry_ref[j] = jnp.zeros(carry_ref.shape[1:], F32)

    z = _dot(a_ref[...], w_ref[...].astype(BF16))
    tm = z.shape[0]
    row = lax.broadcasted_iota(jnp.int32, (tm, 1), 0)
    sublanes = carry_ref.shape[1]
    prev_row = jnp.where(i % m_tiles_per_seq == 0, 0.0, carry_ref[j, sublanes - 1:sublanes, :])
    prev = jnp.where(row == 0, prev_row, pltpu.roll(z, 1, axis=0))
    o_ref[...] = z + mix_ref[...] * (prev - z)
    carry_ref[j] = z[tm - sublanes:tm, :]


def _matmul_token_shift(a, w, mix, seq, *, n, w_col_off, tm=1024, tn=512, name="matmul_shift"):
    m, k = a.shape
    n_tiles = n // tn
    return pl.pallas_call(
        functools.partial(_mm_shift_kernel, m_tiles_per_seq=seq // tm),
        grid=(m // tm, n_tiles),
        in_specs=[pl.BlockSpec((tm, k), lambda i, j: (i, 0)),
                  pl.BlockSpec((k, tn), lambda i, j: (0, w_col_off + j)),
                  pl.BlockSpec((1, tn), lambda i, j: (0, j))],
        out_specs=pl.BlockSpec((tm, tn), lambda i, j: (i, j)),
        out_shape=jax.ShapeDtypeStruct((m, n), F32),
        scratch_shapes=[pltpu.VMEM((n_tiles, SUBLANES, tn), F32)],
        compiler_params=_cparams(("arbitrary", "arbitrary")),
        name=name,
    )(a, w, mix)


def _realign_cast_kernel(a_ref, b_ref, o_ref, *, shift):
    lane = lax.broadcasted_iota(jnp.int32, (1, a_ref.shape[1]), 1)
    o_ref[...] = jnp.where(lane < shift, pltpu.roll(a_ref[...], shift, axis=1),
                           pltpu.roll(b_ref[...], shift, axis=1)).astype(o_ref.dtype)


def _realign_cast(w, col0, n, tr=512, tn=512):
    k = w.shape[0]
    blk0, rem = divmod(col0, tn)
    return pl.pallas_call(
        functools.partial(_realign_cast_kernel, shift=tn - rem),
        grid=(k // tr, n // tn),
        in_specs=[pl.BlockSpec((tr, tn), lambda i, j: (i, blk0 + j)),
                  pl.BlockSpec((tr, tn), lambda i, j: (i, blk0 + j + 1))],
        out_specs=pl.BlockSpec((tr, tn), lambda i, j: (i, j)),
        out_shape=jax.ShapeDtypeStruct((k, n), BF16),
        compiler_params=_cparams(("parallel", "parallel")),
        name="realign_cast",
    )(w, w)


def _mm_res_kernel(a_ref, w_ref, r_ref, o_ref):
    o_ref[...] = r_ref[...] + _dot(a_ref[...], w_ref[...].astype(BF16))


def _matmul_residual(a, w, res, *, tm=1024, tn=512, name="matmul_res"):
    m, k = a.shape
    n = w.shape[1]
    return pl.pallas_call(
        _mm_res_kernel,
        grid=(m // tm, n // tn),
        in_specs=[
            pl.BlockSpec((tm, k), lambda i, j: (i, 0)),
            pl.BlockSpec((k, tn), lambda i, j: (0, j)),
            pl.BlockSpec((tm, tn), lambda i, j: (i, j)),
        ],
        out_specs=pl.BlockSpec((tm, tn), lambda i, j: (i, j)),
        out_shape=jax.ShapeDtypeStruct((m, n), F32),
        compiler_params=_cparams(("parallel", "arbitrary")),
        name=name,
    )(a, w, res)


def _mm_kacc_res_kernel(a_ref, w_ref, r_ref, o_ref):
    k = pl.program_id(2)

    @pl.when(k == 0)
    def _():
        o_ref[...] = r_ref[...] + _dot(a_ref[...], w_ref[...].astype(BF16))

    @pl.when(k > 0)
    def _():
        o_ref[...] += _dot(a_ref[...], w_ref[...].astype(BF16))


def _matmul_kacc_residual(a, w, res, *, tm=1024, tn=1024, tk=2048, name="matmul_kacc"):
    m, kdim = a.shape
    n = w.shape[1]
    return pl.pallas_call(
        _mm_kacc_res_kernel,
        grid=(m // tm, n // tn, kdim // tk),
        in_specs=[
            pl.BlockSpec((tm, tk), lambda i, j, k: (i, k)),
            pl.BlockSpec((tk, tn), lambda i, j, k: (k, j)),
            pl.BlockSpec((tm, tn), lambda i, j, k: (i, j)),
        ],
        out_specs=pl.BlockSpec((tm, tn), lambda i, j, k: (i, j)),
        out_shape=jax.ShapeDtypeStruct((m, n), F32),
        compiler_params=_cparams(("parallel", "parallel", "arbitrary")),
        name=name,
    )(a, w, res)


def _t5_bucket(dist):
    max_exact = N_BUCKETS // 2
    d_f = jnp.maximum(dist, 1).astype(F32)
    large = max_exact + (jnp.log(d_f / max_exact) / math.log(MAX_DISTANCE / max_exact)
                         * (N_BUCKETS - max_exact)).astype(jnp.int32)
    large = jnp.minimum(large, N_BUCKETS - 1)
    return jnp.where(dist < max_exact, dist, large)


def _band_buckets():
    blk = ATTN_BLOCK
    rel = (blk + jnp.arange(blk))[:, None] - jnp.arange(2 * blk)[None, :]
    out = []
    for window, dilation in DILATED_GROUPS:
        band = (rel >= 0) & (rel <= window // dilation)
        out.append(jnp.where(band, _t5_bucket(jnp.maximum(rel, 0) * dilation), -1))
    return jnp.stack(out).astype(jnp.int32)


def _attn_kernel(q0, k0, v0, q1, k1, v1, q2, k2, v2, bucket_ref, tab_ref, o_ref, oacc_ref, lacc_ref):
    blk = ATTN_BLOCK
    hh = pl.program_id(1)
    seq = oacc_ref.shape[0]
    refs = ((q0, k0, v0), (q1, k1, v1), (q2, k2, v2))

    def softmax_out(s, v):
        m = jnp.max(s, axis=-1, keepdims=True)
        e = jnp.exp(s - m)
        l = jnp.sum(e, axis=-1, keepdims=True)
        o = _dot(e.astype(BF16), v) / l
        return o, jnp.broadcast_to(m + jnp.log(l), (blk, LANES))

    for g, (_, dilation) in enumerate(DILATED_GROUPS):
        q_ref, k_ref, v_ref = refs[g]
        n_blocks = seq // (dilation * blk)
        bucket = bucket_ref[g]
        head = g * ATTN_HEADS_PER_GROUP + hh
        bias = jnp.where(bucket < 0, MASK_VALUE, 0.0)
        for b in range(N_BUCKETS):
            bias = jnp.where(bucket == b, tab_ref[b, head], bias)

        def merge_store(tok_rows, o, lse, g=g):
            if g == 0:
                oacc_ref[tok_rows, :] = o
                lacc_ref[tok_rows, :] = lse
            else:
                o_old = oacc_ref[tok_rows, :]
                l_old = lacc_ref[tok_rows, :]
                m = jnp.maximum(l_old, lse)
                w_old = jnp.exp(l_old - m)
                w_new = jnp.exp(lse - m)
                den = w_old + w_new
                oacc_ref[tok_rows, :] = (w_old * o_old + w_new * o) / den
                lacc_ref[tok_rows, :] = m + jnp.log(den)

        def tok_rows(off, r, dilation=dilation):
            if dilation == 1:
                return pl.ds(off, blk)
            return pl.ds(off * dilation + r, blk, stride=dilation)

        def unit(off, r, q_ref=q_ref, k_ref=k_ref, v_ref=v_ref, bias=bias, merge_store=merge_store,
                 tok_rows=tok_rows):
            cs = slice(r * LANES, (r + 1) * LANES)
            s = _dot_nt(q_ref[pl.ds(off, blk), cs], k_ref[pl.ds(off - blk, 2 * blk), cs]) + bias
            o, lse = softmax_out(s, v_ref[pl.ds(off - blk, 2 * blk), cs])
            merge_store(tok_rows(off, r), o, lse)

        for r in range(dilation):
            cs = slice(r * LANES, (r + 1) * LANES)
            s0 = _dot_nt(q_ref[0:blk, cs], k_ref[0:blk, cs]) + bias[:, blk:]
            o0, lse0 = softmax_out(s0, v_ref[0:blk, cs])
            merge_store(tok_rows(0, r), o0, lse0)

        ways = ATTN_UNITS_PER_STEP
        if dilation == 1:
            part = n_blocks // ways
            for w in range(1, ways):
                unit(w * part * blk, 0)

            def body(n, carry, unit=unit, part=part):
                off = pl.multiple_of(n * blk, blk)
                for w in range(ways):
                    unit(off + w * part * blk, 0)
                return carry

            lax.fori_loop(1, part, body, 0)
        else:
            for r in range(0, dilation, ways):
                def body(n, carry, unit=unit, r=r):
                    off = pl.multiple_of(n * blk, blk)
                    for w in range(ways):
                        unit(off, r + w)
                    return carry

                lax.fori_loop(1, n_blocks, body, 0)

    o_ref[...] = oacc_ref[...].astype(o_ref.dtype)


def _dilated_attention(qkv_groups, rel_bias, batch, seq):
    hg = ATTN_HEADS_PER_GROUP
    in_specs, operands = [], []
    for (_, dilation), arr in zip(DILATED_GROUPS, qkv_groups):
        for section in range(3):
            in_specs.append(pl.BlockSpec((None, None, seq // dilation, dilation * LANES),
                                         lambda b, h, section=section: (b, section * hg + h, 0, 0)))
            operands.append(arr)
    buckets = _band_buckets()
    in_specs.append(pl.BlockSpec(buckets.shape, lambda b, h: (0, 0, 0)))
    in_specs.append(pl.BlockSpec(memory_space=pltpu.SMEM))
    return pl.pallas_call(
        _attn_kernel,
        grid=(batch, hg),
        in_specs=in_specs,
        out_specs=pl.BlockSpec((seq, LANES), lambda b, h: (b, h)),
        out_shape=jax.ShapeDtypeStruct((batch * seq, hg * LANES), BF16),
        scratch_shapes=[pltpu.VMEM((seq, LANES), F32), pltpu.VMEM((seq, LANES), F32)],
        compiler_params=_cparams(("parallel", "parallel")),
        name="dilated_attn",
    )(*operands, buckets, rel_bias.astype(F32))


_STAGE_BF16 = ("at", "rt", "bt", "kt", "bh", "kh", "vb")
_STAGE_F32 = ("egc", "bonus", "gate")
_STAGE_NAMES = _STAGE_BF16 + _STAGE_F32
_N_HANDOFF = 9


def _seg_sum(x, seg):
    w = seg.shape[0]
    return jnp.concatenate([_dot_split2(x[:, i * w:(i + 1) * w], seg) for i in range(x.shape[1] // w)], axis=1)


def _rwkv_prepare(r_ref, k_ref, v_ref, xl_ref, wl_ref, w0_ref, a0_ref, kk_ref, ka_ref, rk_ref, seg_ref, tri_ref,
                  stage, lora_dims):
    tblk, cw = r_ref.shape
    ch = RWKV_CHUNK
    r, k, v, xl = r_ref[...], k_ref[...], v_ref[...], xl_ref[...]

    d_w, d_a, _ = lora_dims
    lane_l = lax.broadcasted_iota(jnp.int32, xl.shape, 1)
    act = jnp.where(lane_l < d_w, jnp.tanh(xl), jnp.where(lane_l < d_w + d_a, xl, jax.nn.sigmoid(xl)))
    up = _dot(act.astype(BF16), wl_ref[...])
    w_arg = -(w0_ref[...] + up[:, 0:cw])
    softplus = jnp.maximum(w_arg, 0.0) + jnp.log1p(jnp.exp(-jnp.abs(w_arg)))
    logdecay = -jnp.exp(-softplus - 0.5)
    a_gate = jax.nn.sigmoid(a0_ref[...] + up[:, cw:2 * cw])
    stage["gate"][...] = up[:, 2 * cw:3 * cw]
    yield

    seg = seg_ref[...]
    kk = k * kk_ref[...]
    kk = kk * lax.rsqrt(jnp.maximum(_seg_sum(kk * kk, seg), 1e-24))
    k2 = k * (1.0 + (a_gate - 1.0) * ka_ref[...])
    a_vec = -kk
    b_vec = kk * a_gate
    stage["bonus"][...] = _seg_sum(r * k2 * rk_ref[...], seg) * v
    yield

    tri = tri_ref[...]
    for c in range(tblk // ch):
        rows = slice(c * ch, (c + 1) * ch)
        ld = logdecay[rows]
        g_inc = _dot_split3_left(tri, ld)
        g_exc = g_inc - ld
        e_neg = jnp.exp(-g_inc)
        e_last = jnp.exp(g_inc[ch - 1:ch, :])
        bt = b_vec[rows] * e_neg
        kt = k2[rows] * e_neg
        stage["at"][rows, :] = (a_vec[rows] * jnp.exp(g_exc)).astype(BF16)
        stage["rt"][rows, :] = (r[rows] * jnp.exp(g_inc)).astype(BF16)
        stage["bt"][rows, :] = bt.astype(BF16)
        stage["kt"][rows, :] = kt.astype(BF16)
        stage["bh"][rows, :] = (bt * e_last).astype(BF16)
        stage["kh"][rows, :] = (kt * e_last).astype(BF16)
        stage["egc"][c] = jnp.broadcast_to(e_last, stage["egc"].shape[1:])
        yield
    stage["vb"][...] = v.astype(BF16)


def _rwkv_solve(stage, state_ref, y_ref, o_ref, gw_ref, gb_ref, seg_ref, handoff, zero_off):
    tblk, cw = o_ref.shape
    ch = RWKV_CHUNK
    n_chunks = tblk // ch
    n_pairs = cw // LANES
    hd = RWKV_HEAD_DIM

    lane = lax.broadcasted_iota(jnp.int32, (1, LANES), 1)
    m0, m1 = lane < hd, lane >= hd
    lane2 = lax.broadcasted_iota(jnp.int32, (1, 2 * LANES), 1) % LANES
    m0w, m1w = lane2 < hd, lane2 >= hd
    rr = lax.broadcasted_iota(jnp.int32, (ch, 2 * LANES), 0)
    cc = lax.broadcasted_iota(jnp.int32, (ch, 2 * LANES), 1) % ch
    strict_lower = cc < rr
    incl_lower = cc <= rr
    eye2 = (lax.broadcasted_iota(jnp.int32, (ch, LANES), 0)
            == lax.broadcasted_iota(jnp.int32, (ch, LANES), 1) % ch).astype(F32)
    bd_r = lax.broadcasted_iota(jnp.int32, (LANES, LANES), 0) >= hd
    bd_c = lax.broadcasted_iota(jnp.int32, (LANES, LANES), 1) >= hd
    block_diag = bd_r == bd_c
    zeros_cv = jnp.zeros((ch, LANES), BF16)

    def keep(mask, x):
        return jnp.where(mask, x, jnp.zeros_like(x))

    problems = [(c, p) for c in range(n_chunks) for p in range(n_pairs)]

    def blk(name, cp):
        c, p = cp
        return stage[name][c * ch:(c + 1) * ch, p * LANES:(p + 1) * LANES]

    aakv_s, tcat_s, g2_s, wu_s, ry_s, rt2_s, y0_s, mlr_s, n0_s = handoff

    g1, pw, tinv = {}, {}, {}
    for i, cp in enumerate(problems):
        bt, kt = blk("bt", cp), blk("kt", cp)
        x = jnp.concatenate([blk("at", cp), blk("rt", cp)], axis=0)
        ym = jnp.concatenate([keep(m0, bt), keep(m0, kt), keep(m1, kt), keep(m1, bt)], axis=0)
        g_all = _dot_nt(x, ym)
        g1[cp] = jnp.where(strict_lower, g_all[:ch], 0.0)
        g2_s[i] = jnp.where(incl_lower, g_all[ch:], 0.0).astype(BF16)
        pw[cp] = jnp.where(m0, g1[cp][:, :LANES], g1[cp][:, LANES:])
        tinv[cp] = eye2 + pw[cp]
    yield

    def block_diag_of(cat):
        cat = cat.astype(BF16)
        return jnp.concatenate([keep(m0, cat), keep(m1, cat)], axis=0)

    n_sq = int(math.log2(ch)) - 1
    for cp in problems:
        pw[cp] = _dot(pw[cp].astype(BF16), block_diag_of(pw[cp]))
    yield
    for _ in range(n_sq - 1):
        for cp in problems:
            both = _dot(jnp.concatenate([tinv[cp], pw[cp]], axis=0).astype(BF16), block_diag_of(pw[cp]))
            tinv[cp] = tinv[cp] + both[:ch]
            pw[cp] = both[ch:]
        yield
    for cp in problems:
        tinv[cp] = tinv[cp] + _dot(tinv[cp].astype(BF16), block_diag_of(pw[cp]))
    yield
    for i, cp in enumerate(problems):
        vb = blk("vb", cp)
        v_stack = jnp.concatenate([zeros_cv, keep(m0, vb), keep(m1, vb), zeros_cv], axis=0)
        aakv_s[i] = _dot(g1[cp].astype(BF16), v_stack).astype(BF16)
        tcat_s[i] = tinv[cp].astype(BF16)
    yield
    for i, cp in enumerate(problems):
        j = zero_off + i
        rhs = jnp.concatenate([blk("at", cp), aakv_s[j]], axis=1)
        wu_s[i] = _dot(tcat_s[j], jnp.concatenate([keep(m0w, rhs), keep(m1w, rhs)], axis=0))
    yield
    for i, cp in enumerate(problems):
        j = zero_off + i
        upper = wu_s[j].astype(BF16)
        lower = jnp.concatenate([zeros_cv, blk("vb", cp)], axis=1)
        rhs = jnp.concatenate([keep(m0w, upper), keep(m0w, lower), keep(m1w, lower), keep(m1w, upper)], axis=0)
        ry_s[i] = _dot(g2_s[j], rhs)
    yield
    for i, cp in enumerate(problems):
        j = zero_off + i
        wu, ry = wu_s[j], ry_s[j]
        bkh = jnp.concatenate([blk("bh", cp), blk("kh", cp)], axis=0)
        rt2_s[i] = (blk("rt", cp).astype(F32) + ry[:, :LANES]).astype(BF16)
        y0_s[i] = ry[:, LANES:]
        wt_t = wu[:, :LANES].T.astype(BF16)
        mlr_s[i] = jnp.where(block_diag, _dot(wt_t, bkh[:ch]), 0.0).astype(BF16)
        uv_t = jnp.concatenate([wu[:, LANES:], blk("vb", cp).astype(F32)], axis=0).T.astype(BF16)
        n0_s[i] = jnp.where(block_diag, _dot(uv_t, bkh), 0.0)
    yield

    for c in range(n_chunks):
        rows = slice(c * ch, (c + 1) * ch)
        for p in range(n_pairs):
            cs = slice(p * LANES, (p + 1) * LANES)
            j = zero_off + (c * n_pairs + p)
            ht = state_ref[p]
            htb = ht.astype(BF16)
            y_ref[rows, cs] = _dot_nt(rt2_s[j], htb) + y0_s[j]
            state_ref[p] = ht * stage["egc"][c, 0:1, cs] + _dot(htb, mlr_s[j]) + n0_s[j]
    yield

    seg = seg_ref[...]
    y = y_ref[...]
    inv_hd = 1.0 / hd
    mu = _seg_sum(y, seg) * inv_hd
    yc = y - mu
    var = _seg_sum(yc * yc, seg) * inv_hd
    yn = yc * lax.rsqrt(var + GN_EPS) * gw_ref[...] + gb_ref[...]
    o_ref[...] = ((yn + stage["bonus"][...]) * stage["gate"][...]).astype(o_ref.dtype)


def _interleave(streams, order):
    for i in order:
        next(streams[i], None)
    for g in streams:
        for _ in g:
            pass


_RWKV_TRACE_ORDER = (0, 1, 0, 0, 1, 0, 0, 0, 0, 0, 1, 0, 1, 0, 1, 0, 1, 0, 0)


def _rwkv_kernel(*refs, lora_dims):
    n_in = 15
    (r_ref, k_ref, v_ref, xl_ref, wl_ref, w0_ref, a0_ref, kk_ref, ka_ref, rk_ref, gw_ref, gb_ref, seg_ref,
     tri_ref, zero_ref) = refs[:n_in]
    o_ref = refs[n_in]
    scratch = refs[n_in + 1:]
    state_ref, y_ref = scratch[:2]
    handoff = scratch[2:2 + _N_HANDOFF]
    ns = len(_STAGE_NAMES)
    stages = [dict(zip(_STAGE_NAMES, scratch[2 + _N_HANDOFF + i * ns: 2 + _N_HANDOFF + (i + 1) * ns]))
              for i in range(2)]
    t = pl.program_id(2)

    @pl.when(t == 0)
    def _():
        state_ref[...] = jnp.zeros_like(state_ref)
        for ref in stages[1].values():
            ref[...] = jnp.zeros_like(ref)

    for parity in range(2):
        @pl.when(t % 2 == parity)
        def _(parity=parity):
            _interleave([
                _rwkv_solve(stages[1 - parity], state_ref, y_ref, o_ref, gw_ref, gb_ref, seg_ref, handoff,
                            zero_ref[0]),
                _rwkv_prepare(r_ref, k_ref, v_ref, xl_ref, wl_ref, w0_ref, a0_ref, kk_ref, ka_ref, rk_ref,
                              seg_ref, tri_ref, stages[parity], lora_dims)], _RWKV_TRACE_ORDER)


def _rwkv7(zr, zl, w0, w_decay_up, a0, w_aaa_up, w_gate_up, k_k, k_a, r_k, gn_w, gn_b, batch, seq):
    tok = zr.shape[0]
    width = zr.shape[1] // 3
    lw = zl.shape[1]
    cw = RWKV_HPB * RWKV_HEAD_DIM
    nhb = width // cw
    tblk = RWKV_TBLK
    nt = seq // tblk
    ch = RWKV_CHUNK

    def pad_rows(w, start):
        return jnp.pad(w, ((start, lw - start - w.shape[0]), (0, 0)))

    wl = jnp.stack([pad_rows(w_decay_up, 0), pad_rows(w_aaa_up, DECAY_LORA),
                    pad_rows(w_gate_up, DECAY_LORA + AAA_LORA)], axis=0)
    wl = wl.reshape(3, lw, nhb, cw).transpose(2, 1, 0, 3).reshape(nhb, lw, 3 * cw).astype(BF16)

    idx = jnp.arange(MXU_DIM) // RWKV_HEAD_DIM
    seg = (idx[:, None] == idx[None, :]).astype(BF16)
    tri = (jnp.arange(ch)[:, None] >= jnp.arange(ch)[None, :]).astype(BF16)

    def in_blk(b, t):
        return b * nt + jnp.minimum(t, nt - 1)

    def col(sec):
        return pl.BlockSpec((tblk, cw), lambda b, h, t: (in_blk(b, t), sec * nhb + h))

    vec = pl.BlockSpec((1, cw), lambda b, h, t: (0, h))
    row_vec = lambda a: a.reshape(1, width)
    stage_shapes = ([pltpu.VMEM((tblk, cw), BF16)] * len(_STAGE_BF16)
                    + [pltpu.VMEM((tblk // ch, 8, cw), F32),
                       pltpu.VMEM((tblk, cw), F32), pltpu.VMEM((tblk, cw), F32)])
    n_prob = (tblk // ch) * (cw // LANES)
    handoff_shapes = [
        pltpu.VMEM((n_prob, ch, LANES), BF16),
        pltpu.VMEM((n_prob, ch, LANES), BF16),
        pltpu.VMEM((n_prob, ch, 2 * LANES), BF16),
        pltpu.VMEM((n_prob, ch, 2 * LANES), F32),
        pltpu.VMEM((n_prob, ch, 2 * LANES), F32),
        pltpu.VMEM((n_prob, ch, LANES), BF16), pltpu.VMEM((n_prob, ch, LANES), F32),
        pltpu.VMEM((n_prob, LANES, LANES), BF16), pltpu.VMEM((n_prob, LANES, LANES), F32),
    ]
    assert len(handoff_shapes) == _N_HANDOFF
    kern = functools.partial(_rwkv_kernel, lora_dims=(DECAY_LORA, AAA_LORA, GATE_LORA))
    return pl.pallas_call(
        kern,
        grid=(batch, nhb, nt + 1),
        in_specs=[
            col(0), col(1), col(2),
            pl.BlockSpec((tblk, lw), lambda b, h, t: (in_blk(b, t), 0)),
            pl.BlockSpec((None, lw, 3 * cw), lambda b, h, t: (h, 0, 0)),
            vec, vec, vec, vec, vec, vec, vec,
            pl.BlockSpec((MXU_DIM, MXU_DIM), lambda b, h, t: (0, 0)),
            pl.BlockSpec((ch, ch), lambda b, h, t: (0, 0)),
            pl.BlockSpec(memory_space=pltpu.SMEM),
        ],
        out_specs=pl.BlockSpec((tblk, cw), lambda b, h, t: (b * nt + jnp.maximum(t - 1, 0), h)),
        out_shape=jax.ShapeDtypeStruct((tok, width), BF16),
        scratch_shapes=[
            pltpu.VMEM((cw // LANES, LANES, LANES), F32),
            pltpu.VMEM((tblk, cw), F32),
        ] + handoff_shapes + stage_shapes * 2,
        compiler_params=_cparams(("parallel", "parallel", "arbitrary")),
        name="rwkv7_chunked",
    )(zr, zr, zr, zl, wl,
      row_vec(w0), row_vec(a0), row_vec(k_k), row_vec(k_a), row_vec(r_k), row_vec(gn_w), row_vec(gn_b),
      seg, tri, jnp.zeros((1,), jnp.int32))


def _merge_kernel(attn_ref, rw_ref, g0_ref, g1_ref, wa_ref, wr_ref, out_ref):
    attn_d = _dot(attn_ref[...], wa_ref[...].astype(BF16))
    rwkv_d = _dot(rw_ref[...], wr_ref[...].astype(BF16))
    out_ref[...] = (g0_ref[...].astype(F32) * attn_d + g1_ref[...].astype(F32) * rwkv_d).astype(out_ref.dtype)


def _merge(attn, rw, gates, w_attn_up, w_rwkv_up, tm=1024, tn=512):
    tok = rw.shape[0]
    d = w_attn_up.shape[1]
    nj = d // tn
    return pl.pallas_call(
        _merge_kernel,
        grid=(tok // tm, nj),
        in_specs=[
            pl.BlockSpec((tm, attn.shape[1]), lambda i, j: (i, 0)),
            pl.BlockSpec((tm, rw.shape[1]), lambda i, j: (i, 0)),
            pl.BlockSpec((tm, tn), lambda i, j: (i, j)),
            pl.BlockSpec((tm, tn), lambda i, j: (i, nj + j)),
            pl.BlockSpec((w_attn_up.shape[0], tn), lambda i, j: (0, j)),
            pl.BlockSpec((w_rwkv_up.shape[0], tn), lambda i, j: (0, j)),
        ],
        out_specs=pl.BlockSpec((tm, tn), lambda i, j: (i, j)),
        out_shape=jax.ShapeDtypeStruct((tok, d), BF16),
        compiler_params=_cparams(("parallel", "arbitrary")),
        name="merge_gated_up",
    )(attn, rw, gates, gates, w_attn_up, w_rwkv_up)


def _ple_kernel(h_ref, wg_ref, p_ref, wp_ref, x_ref, o_ref):
    gate = jax.nn.sigmoid(_dot(h_ref[...], wg_ref[...].astype(BF16)))
    o_ref[...] = x_ref[...] + gate * _dot(p_ref[...], wp_ref[...].astype(BF16))


def _ple(hn, w_gate, p, w_proj, x, tm=1024, tn=512):
    m, k = hn.shape
    n = w_gate.shape[1]
    kp = p.shape[1]
    return pl.pallas_call(
        _ple_kernel,
        grid=(m // tm, n // tn),
        in_specs=[
            pl.BlockSpec((tm, k), lambda i, j: (i, 0)),
            pl.BlockSpec((k, tn), lambda i, j: (0, j)),
            pl.BlockSpec((tm, kp), lambda i, j: (i, 0)),
            pl.BlockSpec((kp, tn), lambda i, j: (0, j)),
            pl.BlockSpec((tm, tn), lambda i, j: (i, j)),
        ],
        out_specs=pl.BlockSpec((tm, tn), lambda i, j: (i, j)),
        out_shape=jax.ShapeDtypeStruct((m, n), F32),
        compiler_params=_cparams(("parallel", "arbitrary")),
        name="ple_gated",
    )(hn, w_gate, p, w_proj, x)


def _layer(x2d, p2d, batch, seq, norm_mix, w_in, q_gain, k_gain, rel_bias, w_attn_up, shift_mix, w0,
           w_decay_up, a0, w_aaa_up, w_gate_up, k_k, k_a, r_k, gn_w, gn_b, w_rwkv_up, w_out, norm_mlp,
           w_mlp_in, w_mlp_out, norm_ple, w_ple_gate, w_ple_proj):
    n_attn_heads = len(DILATED_GROUPS) * ATTN_HEADS_PER_GROUP
    attn_width = n_attn_heads * ATTN_HEAD_DIM
    rwkv_width = w_rwkv_up.shape[0]
    a_end = 3 * attn_width
    z_end = a_end + 3 * rwkv_width
    r_end = z_end + DECAY_LORA + AAA_LORA + GATE_LORA

    h = _rmsnorm_bf16(x2d, norm_mix)

    gains = jnp.stack([q_gain * (ATTN_HEAD_DIM ** -0.5), k_gain, jnp.ones_like(q_gain)]).reshape(3, 1, LANES)
    tn = ATTN_HEADS_PER_GROUP * ATTN_HEAD_DIM
    qkv_groups = [_qkv_proj(h, w_in, gains, gi, dilation, batch, seq)
                  for gi, (_, dilation) in enumerate(DILATED_GROUPS)]
    mix_z = shift_mix[: z_end - a_end].reshape(1, -1)
    zr = _matmul_token_shift(h, w_in, mix_z, seq, n=z_end - a_end, w_col_off=a_end // tn, tn=tn, name="w_in_rkv")
    mix_l = jnp.pad(shift_mix[z_end - a_end:], (0, tn - (r_end - z_end))).reshape(1, tn)
    zl = _matmul_token_shift(h, w_in, mix_l, seq, n=tn, w_col_off=z_end // tn, tn=tn, name="w_in_lora")
    w_gates = _realign_cast(w_in, r_end, w_in.shape[1] - r_end)
    gates = _matmul(h, w_gates, act="sigmoid", out_dtype=BF16, tn=1024, name="w_in_gates")

    attn = _dilated_attention(qkv_groups, rel_bias, batch, seq)

    rw = _rwkv7(zr, zl, w0, w_decay_up, a0, w_aaa_up, w_gate_up, k_k, k_a, r_k, gn_w, gn_b, batch, seq)

    merged = _merge(attn, rw, gates, w_attn_up, w_rwkv_up)
    x1 = _matmul_residual(merged, w_out, x2d, name="w_out_res")

    h2 = _rmsnorm_bf16(x1, norm_mlp)
    u = _matmul(h2, w_mlp_in, act="relu2", out_dtype=BF16, name="mlp_in")
    x2 = _matmul_kacc_residual(u, w_mlp_out, x1, name="mlp_out_res")

    hn = _rmsnorm_bf16(x2, norm_ple)
    return _ple(hn, w_ple_gate, p2d.astype(BF16), w_ple_proj, x2)


def kernel(x, p, norm_mix, w_in, q_gain, k_gain, rel_bias, w_attn_up, shift_mix, w0, w_decay_up, a0, w_aaa_up,
           w_gate_up, k_k, k_a, r_k, gn_w, gn_b, w_rwkv_up, w_out, norm_mlp, w_mlp_in, w_mlp_out, norm_ple,
           w_ple_gate, w_ple_proj):
    batch, seq, d_model = x.shape
    depth = p.shape[0]
    x2d = x.reshape(batch * seq, d_model)
    for i in range(depth):
        x2d = _layer(x2d, p[i].reshape(batch * seq, -1), batch, seq, norm_mix[i], w_in[i], q_gain[i], k_gain[i],
                     rel_bias, w_attn_up[i], shift_mix[i], w0[i], w_decay_up[i], a0[i], w_aaa_up[i],
                     w_gate_up[i], k_k[i], k_a[i], r_k[i], gn_w[i], gn_b[i], w_rwkv_up[i], w_out[i],
                     norm_mlp[i], w_mlp_in[i], w_mlp_out[i], norm_ple[i], w_ple_gate[i], w_ple_proj[i])
    return x2d.reshape(batch, seq, d_model)
```

```python
import functools
import math

import jax
import jax.numpy as jnp
from jax import lax
from jax.experimental import pallas as pl
from jax.experimental.pallas import tpu as pltpu

F32 = jnp.float32
BF16 = jnp.bfloat16

LANES = 128
SUBLANES = 8
MXU_DIM = 256
ATTN_HEAD_DIM = 128
ATTN_HEADS_PER_GROUP = 4
DILATED_GROUPS = ((128, 1), (512, 4), (2048, 16))
ATTN_BLOCK = 128
ATTN_UNITS_PER_STEP = 4
N_BUCKETS = 32
MAX_DISTANCE = 2048
RWKV_HEAD_DIM = 64
DECAY_LORA = 96
AAA_LORA = 96
GATE_LORA = 256
RMS_EPS = 1e-6
GN_EPS = 64e-5
MASK_VALUE = -1e30

RWKV_CHUNK = 64
RWKV_TBLK = 256
RWKV_HPB = 8

VMEM_LIMIT = 56 * 1024 * 1024


def _cparams(sem):
    return pltpu.CompilerParams(dimension_semantics=sem, vmem_limit_bytes=VMEM_LIMIT)


def _dot(a, b):
    return jnp.dot(a, b, preferred_element_type=F32)


def _dot_nt(a, b):
    return lax.dot_general(a, b, (((1,), (1,)), ((), ())), preferred_element_type=F32)


def _dot_split2(x, w_bf16):
    hi = x.astype(BF16)
    lo = (x - hi.astype(F32)).astype(BF16)
    return _dot(hi, w_bf16) + _dot(lo, w_bf16)


def _dot_split3_left(w_bf16, x):
    hi = x.astype(BF16)
    r1 = x - hi.astype(F32)
    mid = r1.astype(BF16)
    lo = (r1 - mid.astype(F32)).astype(BF16)
    return _dot(w_bf16, hi) + _dot(w_bf16, mid) + _dot(w_bf16, lo)


def _rmsnorm_kernel(x_ref, g_ref, o_ref):
    x = x_ref[...]
    ms = jnp.mean(x * x, axis=-1, keepdims=True)
    o_ref[...] = (x * lax.rsqrt(ms + RMS_EPS) * g_ref[...]).astype(o_ref.dtype)


def _rmsnorm_bf16(x, gain, tr=256):
    m, d = x.shape
    return pl.pallas_call(
        _rmsnorm_kernel,
        grid=(m // tr,),
        in_specs=[pl.BlockSpec((tr, d), lambda i: (i, 0)), pl.BlockSpec((1, d), lambda i: (0, 0))],
        out_specs=pl.BlockSpec((tr, d), lambda i: (i, 0)),
        out_shape=jax.ShapeDtypeStruct((m, d), BF16),
        compiler_params=_cparams(("parallel",)),
        name="rmsnorm",
    )(x, gain.reshape(1, d))


def _qkv_kernel(a_ref, w_ref, g_ref, o_ref, scr_ref, *, dilation):
    j = pl.program_id(1)
    acc = _dot_nt(a_ref[...], w_ref[...].astype(BF16))
    rows = acc.shape[0] // dilation
    for h in range(ATTN_HEADS_PER_GROUP):
        y = acc[:, h * LANES:(h + 1) * LANES]
        ms = jnp.mean(y * y, axis=-1, keepdims=True)
        y = y * jnp.where(j < 2, lax.rsqrt(ms + RMS_EPS), 1.0) * g_ref[...]
        if dilation == 1:
            o_ref[h] = y.astype(o_ref.dtype)
        else:
            scr_ref[h] = y
            for r in range(dilation):
                o_ref[h, :, r * LANES:(r + 1) * LANES] = (
                    scr_ref[h, pl.ds(r, rows, stride=dilation), :].astype(o_ref.dtype))


def _qkv_proj(h, w_in_t, gains, group, dilation, batch, seq, tm=1024):
    m, d = h.shape
    tn = ATTN_HEADS_PER_GROUP * LANES
    mt_per_b = seq // tm
    hg = ATTN_HEADS_PER_GROUP
    n_groups = len(DILATED_GROUPS)
    return pl.pallas_call(
        functools.partial(_qkv_kernel, dilation=dilation),
        grid=(m // tm, 3),
        in_specs=[
            pl.BlockSpec((tm, d), lambda i, j: (i, 0)),
            pl.BlockSpec((tn, d), lambda i, j: (j * n_groups + group, 0)),
            pl.BlockSpec((None, 1, LANES), lambda i, j: (j, 0, 0)),
        ],
        out_specs=pl.BlockSpec((None, hg, tm // dilation, dilation * LANES),
                               lambda i, j: (i // mt_per_b, j, i % mt_per_b, 0)),
        out_shape=jax.ShapeDtypeStruct((batch, 3 * hg, seq // dilation, dilation * LANES), BF16),
        scratch_shapes=[pltpu.VMEM((hg, tm, LANES), F32)],
        compiler_params=_cparams(("parallel", "arbitrary")),
        name=f"w_in_qkv_d{dilation}",
    )(h, w_in_t, gains)


def _mm_kernel(a_ref, w_ref, o_ref, *, act, transposed_w):
    w = w_ref[...].astype(BF16)
    acc = _dot_nt(a_ref[...], w) if transposed_w else _dot(a_ref[...], w)
    if act == "sigmoid":
        acc = jax.nn.sigmoid(acc)
    elif act == "relu2":
        acc = jnp.square(jnp.maximum(acc, 0.0))
    o_ref[...] = acc.astype(o_ref.dtype)


def _matmul(a, w, *, n=None, wt_row0=None, act=None, out_dtype=F32, tm=1024, tn=512, name="matmul"):
    m, k = a.shape
    transposed_w = wt_row0 is not None
    n = w.shape[1] if n is None else n
    if transposed_w:
        assert wt_row0 % SUBLANES == 0 and tn % SUBLANES == 0
        w_spec = pl.BlockSpec((pl.Element(tn), pl.Element(k)),
                              lambda i, j: (pl.multiple_of(wt_row0 + j * tn, SUBLANES), 0))
    else:
        w_spec = pl.BlockSpec((k, tn), lambda i, j: (0, j))
    return pl.pallas_call(
        functools.partial(_mm_kernel, act=act, transposed_w=transposed_w),
        grid=(m // tm, n // tn),
        in_specs=[pl.BlockSpec((tm, k), lambda i, j: (i, 0)), w_spec],
        out_specs=pl.BlockSpec((tm, tn), lambda i, j: (i, j)),
        out_shape=jax.ShapeDtypeStruct((m, n), out_dtype),
        compiler_params=_cparams(("parallel", "arbitrary")),
        name=name,
    )(a, w)


def _mm_shift_kernel(a_ref, w_ref, mix_ref, o_ref, carry_ref, *, m_tiles_per_seq):
    i = pl.program_id(0)
    j = pl.program_id(1)

    @pl.when(i == 0)
    def _():
        carry_ref[j] = jnp.zeros(carry_ref.shape[1:], F32)

    z = _dot_nt(a_ref[...], w_ref[...].astype(BF16))
    tm = z.shape[0]
    row = lax.broadcasted_iota(jnp.int32, (tm, 1), 0)
    sublanes = carry_ref.shape[1]
    prev_row = jnp.where(i % m_tiles_per_seq == 0, 0.0, carry_ref[j, sublanes - 1:sublanes, :])
    prev = jnp.where(row == 0, prev_row, pltpu.roll(z, 1, axis=0))
    o_ref[...] = z + mix_ref[...] * (prev - z)
    carry_ref[j] = z[tm - sublanes:tm, :]


def _matmul_token_shift(a, wt, mix, seq, *, n, wt_row_blk, tm=1024, tn=512, name="matmul_shift"):
    m, k = a.shape
    n_tiles = n // tn
    return pl.pallas_call(
        functools.partial(_mm_shift_kernel, m_tiles_per_seq=seq // tm),
        grid=(m // tm, n_tiles),
        in_specs=[pl.BlockSpec((tm, k), lambda i, j: (i, 0)),
                  pl.BlockSpec((tn, k), lambda i, j: (wt_row_blk + j, 0)),
                  pl.BlockSpec((1, tn), lambda i, j: (0, j))],
        out_specs=pl.BlockSpec((tm, tn), lambda i, j: (i, j)),
        out_shape=jax.ShapeDtypeStruct((m, n), F32),
        scratch_shapes=[pltpu.VMEM((n_tiles, SUBLANES, tn), F32)],
        compiler_params=_cparams(("arbitrary", "arbitrary")),
        name=name,
    )(a, wt, mix)


def _mm_res_kernel(a_ref, w_ref, r_ref, o_ref):
    o_ref[...] = r_ref[...] + _dot(a_ref[...], w_ref[...].astype(BF16))


def _matmul_residual(a, w, res, *, tm=1024, tn=512, name="matmul_res"):
    m, k = a.shape
    n = w.shape[1]
    return pl.pallas_call(
        _mm_res_kernel,
        grid=(m // tm, n // tn),
        in_specs=[
            pl.BlockSpec((tm, k), lambda i, j: (i, 0)),
            pl.BlockSpec((k, tn), lambda i, j: (0, j)),
            pl.BlockSpec((tm, tn), lambda i, j: (i, j)),
        ],
        out_specs=pl.BlockSpec((tm, tn), lambda i, j: (i, j)),
        out_shape=jax.ShapeDtypeStruct((m, n), F32),
        compiler_params=_cparams(("parallel", "arbitrary")),
        name=name,
    )(a, w, res)


def _mm_kacc_res_kernel(a_ref, w_ref, r_ref, o_ref):
    k = pl.program_id(2)

    @pl.when(k == 0)
    def _():
        o_ref[...] = r_ref[...] + _dot(a_ref[...], w_ref[...].astype(BF16))

    @pl.when(k > 0)
    def _():
        o_ref[...] += _dot(a_ref[...], w_ref[...].astype(BF16))


def _matmul_kacc_residual(a, w, res, *, tm=1024, tn=1024, tk=2048, name="matmul_kacc"):
    m, kdim = a.shape
    n = w.shape[1]
    return pl.pallas_call(
        _mm_kacc_res_kernel,
        grid=(m // tm, n // tn, kdim // tk),
        in_specs=[
            pl.BlockSpec((tm, tk), lambda i, j, k: (i, k)),
            pl.BlockSpec((tk, tn), lambda i, j, k: (k, j)),
            pl.BlockSpec((tm, tn), lambda i, j, k: (i, j)),
        ],
        out_specs=pl.BlockSpec((tm, tn), lambda i, j, k: (i, j)),
        out_shape=jax.ShapeDtypeStruct((m, n), F32),
        compiler_params=_cparams(("parallel", "parallel", "arbitrary")),
        name=name,
    )(a, w, res)


def _t5_bucket(dist):
    max_exact = N_BUCKETS // 2
    d_f = jnp.maximum(dist, 1).astype(F32)
    large = max_exact + (jnp.log(d_f / max_exact) / math.log(MAX_DISTANCE / max_exact)
                         * (N_BUCKETS - max_exact)).astype(jnp.int32)
    large = jnp.minimum(large, N_BUCKETS - 1)
    return jnp.where(dist < max_exact, dist, large)


def _band_buckets():
    blk = ATTN_BLOCK
    rel = (blk + jnp.arange(blk))[:, None] - jnp.arange(2 * blk)[None, :]
    out = []
    for window, dilation in DILATED_GROUPS:
        band = (rel >= 0) & (rel <= window // dilation)
        out.append(jnp.where(band, _t5_bucket(jnp.maximum(rel, 0) * dilation), -1))
    return jnp.stack(out).astype(jnp.int32)


def _attn_kernel(q0, k0, v0, q1, k1, v1, q2, k2, v2, bucket_ref, tab_ref, o_ref, oacc_ref, lacc_ref):
    blk = ATTN_BLOCK
    hh = pl.program_id(1)
    seq = oacc_ref.shape[0]
    refs = ((q0, k0, v0), (q1, k1, v1), (q2, k2, v2))

    def softmax_out(s, v):
        m = jnp.max(s, axis=-1, keepdims=True)
        e = jnp.exp(s - m)
        l = jnp.sum(e, axis=-1, keepdims=True)
        o = _dot(e.astype(BF16), v) / l
        return o, jnp.broadcast_to(m + jnp.log(l), (blk, LANES))

    for g, (_, dilation) in enumerate(DILATED_GROUPS):
        q_ref, k_ref, v_ref = refs[g]
        n_blocks = seq // (dilation * blk)
        bucket = bucket_ref[g]
        head = g * ATTN_HEADS_PER_GROUP + hh
        bias = jnp.where(bucket < 0, MASK_VALUE, 0.0)
        for b in range(N_BUCKETS):
            bias = jnp.where(bucket == b, tab_ref[b, head], bias)

        def merge_store(tok_rows, o, lse, g=g):
            if g == 0:
                oacc_ref[tok_rows, :] = o
                lacc_ref[tok_rows, :] = lse
            else:
                o_old = oacc_ref[tok_rows, :]
                l_old = lacc_ref[tok_rows, :]
                m = jnp.maximum(l_old, lse)
                w_old = jnp.exp(l_old - m)
                w_new = jnp.exp(lse - m)
                den = w_old + w_new
                oacc_ref[tok_rows, :] = (w_old * o_old + w_new * o) / den
                lacc_ref[tok_rows, :] = m + jnp.log(den)

        def tok_rows(off, r, dilation=dilation):
            if dilation == 1:
                return pl.ds(off, blk)
            return pl.ds(off * dilation + r, blk, stride=dilation)

        def unit(off, r, q_ref=q_ref, k_ref=k_ref, v_ref=v_ref, bias=bias, merge_store=merge_store,
                 tok_rows=tok_rows):
            cs = slice(r * LANES, (r + 1) * LANES)
            s = _dot_nt(q_ref[pl.ds(off, blk), cs], k_ref[pl.ds(off - blk, 2 * blk), cs]) + bias
            o, lse = softmax_out(s, v_ref[pl.ds(off - blk, 2 * blk), cs])
            merge_store(tok_rows(off, r), o, lse)

        for r in range(dilation):
            cs = slice(r * LANES, (r + 1) * LANES)
            s0 = _dot_nt(q_ref[0:blk, cs], k_ref[0:blk, cs]) + bias[:, blk:]
            o0, lse0 = softmax_out(s0, v_ref[0:blk, cs])
            merge_store(tok_rows(0, r), o0, lse0)

        ways = ATTN_UNITS_PER_STEP
        if dilation == 1:
            part = n_blocks // ways
            for w in range(1, ways):
                unit(w * part * blk, 0)

            def body(n, carry, unit=unit, part=part):
                off = pl.multiple_of(n * blk, blk)
                for w in range(ways):
                    unit(off + w * part * blk, 0)
                return carry

            lax.fori_loop(1, part, body, 0)
        else:
            for r in range(0, dilation, ways):
                def body(n, carry, unit=unit, r=r):
                    off = pl.multiple_of(n * blk, blk)
                    for w in range(ways):
                        unit(off, r + w)
                    return carry

                lax.fori_loop(1, n_blocks, body, 0)

    o_ref[...] = oacc_ref[...].astype(o_ref.dtype)


def _dilated_attention(qkv_groups, rel_bias, batch, seq):
    hg = ATTN_HEADS_PER_GROUP
    in_specs, operands = [], []
    for (_, dilation), arr in zip(DILATED_GROUPS, qkv_groups):
        for section in range(3):
            in_specs.append(pl.BlockSpec((None, None, seq // dilation, dilation * LANES),
                                         lambda b, h, section=section: (b, section * hg + h, 0, 0)))
            operands.append(arr)
    buckets = _band_buckets()
    in_specs.append(pl.BlockSpec(buckets.shape, lambda b, h: (0, 0, 0)))
    in_specs.append(pl.BlockSpec(memory_space=pltpu.SMEM))
    return pl.pallas_call(
        _attn_kernel,
        grid=(batch, hg),
        in_specs=in_specs,
        out_specs=pl.BlockSpec((seq, LANES), lambda b, h: (b, h)),
        out_shape=jax.ShapeDtypeStruct((batch * seq, hg * LANES), BF16),
        scratch_shapes=[pltpu.VMEM((seq, LANES), F32), pltpu.VMEM((seq, LANES), F32)],
        compiler_params=_cparams(("parallel", "parallel")),
        name="dilated_attn",
    )(*operands, buckets, rel_bias.astype(F32))


_STAGE_BF16 = ("at", "rt", "bt", "kt", "bh", "kh", "vb")
_STAGE_F32 = ("egc", "bonus", "gate")
_STAGE_NAMES = _STAGE_BF16 + _STAGE_F32
_N_HANDOFF = 9


def _seg_sum(x, seg):
    w = seg.shape[0]
    return jnp.concatenate([_dot_split2(x[:, i * w:(i + 1) * w], seg) for i in range(x.shape[1] // w)], axis=1)


def _rwkv_prepare(r_ref, k_ref, v_ref, xl_ref, wl_ref, w0_ref, a0_ref, kk_ref, ka_ref, rk_ref, seg_ref, tri_ref,
                  stage, lora_dims):
    tblk, cw = r_ref.shape
    ch = RWKV_CHUNK
    r, k, v, xl = r_ref[...], k_ref[...], v_ref[...], xl_ref[...]

    d_w, d_a, _ = lora_dims
    lane_l = lax.broadcasted_iota(jnp.int32, xl.shape, 1)
    act = jnp.where(lane_l < d_w, jnp.tanh(xl), jnp.where(lane_l < d_w + d_a, xl, jax.nn.sigmoid(xl)))
    up = _dot(act.astype(BF16), wl_ref[...])
    w_arg = -(w0_ref[...] + up[:, 0:cw])
    softplus = jnp.maximum(w_arg, 0.0) + jnp.log1p(jnp.exp(-jnp.abs(w_arg)))
    logdecay = -jnp.exp(-softplus - 0.5)
    a_gate = jax.nn.sigmoid(a0_ref[...] + up[:, cw:2 * cw])
    stage["gate"][...] = up[:, 2 * cw:3 * cw]
    yield

    seg = seg_ref[...]
    kk = k * kk_ref[...]
    kk = kk * lax.rsqrt(jnp.maximum(_seg_sum(kk * kk, seg), 1e-24))
    k2 = k * (1.0 + (a_gate - 1.0) * ka_ref[...])
    a_vec = -kk
    b_vec = kk * a_gate
    stage["bonus"][...] = _seg_sum(r * k2 * rk_ref[...], seg) * v
    yield

    tri = tri_ref[...]
    for c in range(tblk // ch):
        rows = slice(c * ch, (c + 1) * ch)
        ld = logdecay[rows]
        g_inc = _dot_split3_left(tri, ld)
        g_exc = g_inc - ld
        e_neg = jnp.exp(-g_inc)
        e_last = jnp.exp(g_inc[ch - 1:ch, :])
        bt = b_vec[rows] * e_neg
        kt = k2[rows] * e_neg
        stage["at"][rows, :] = (a_vec[rows] * jnp.exp(g_exc)).astype(BF16)
        stage["rt"][rows, :] = (r[rows] * jnp.exp(g_inc)).astype(BF16)
        stage["bt"][rows, :] = bt.astype(BF16)
        stage["kt"][rows, :] = kt.astype(BF16)
        stage["bh"][rows, :] = (bt * e_last).astype(BF16)
        stage["kh"][rows, :] = (kt * e_last).astype(BF16)
        stage["egc"][c] = jnp.broadcast_to(e_last, stage["egc"].shape[1:])
        yield
    stage["vb"][...] = v.astype(BF16)


def _rwkv_solve(stage, state_ref, y_ref, o_ref, gw_ref, gb_ref, seg_ref, handoff, zero_off):
    tblk, cw = o_ref.shape
    ch = RWKV_CHUNK
    n_chunks = tblk // ch
    n_pairs = cw // LANES
    hd = RWKV_HEAD_DIM

    lane = lax.broadcasted_iota(jnp.int32, (1, LANES), 1)
    m0, m1 = lane < hd, lane >= hd
    lane2 = lax.broadcasted_iota(jnp.int32, (1, 2 * LANES), 1) % LANES
    m0w, m1w = lane2 < hd, lane2 >= hd
    rr = lax.broadcasted_iota(jnp.int32, (ch, 2 * LANES), 0)
    cc = lax.broadcasted_iota(jnp.int32, (ch, 2 * LANES), 1) % ch
    strict_lower = cc < rr
    incl_lower = cc <= rr
    eye2 = (lax.broadcasted_iota(jnp.int32, (ch, LANES), 0)
            == lax.broadcasted_iota(jnp.int32, (ch, LANES), 1) % ch).astype(F32)
    bd_r = lax.broadcasted_iota(jnp.int32, (LANES, LANES), 0) >= hd
    bd_c = lax.broadcasted_iota(jnp.int32, (LANES, LANES), 1) >= hd
    block_diag = bd_r == bd_c
    zeros_cv = jnp.zeros((ch, LANES), BF16)

    def keep(mask, x):
        return jnp.where(mask, x, jnp.zeros_like(x))

    problems = [(c, p) for c in range(n_chunks) for p in range(n_pairs)]

    def blk(name, cp):
        c, p = cp
        return stage[name][c * ch:(c + 1) * ch, p * LANES:(p + 1) * LANES]

    aakv_s, tcat_s, g2_s, wu_s, ry_s, rt2_s, y0_s, mlr_s, n0_s = handoff

    g1, pw, tinv = {}, {}, {}
    for i, cp in enumerate(problems):
        bt, kt = blk("bt", cp), blk("kt", cp)
        x = jnp.concatenate([blk("at", cp), blk("rt", cp)], axis=0)
        ym = jnp.concatenate([keep(m0, bt), keep(m0, kt), keep(m1, kt), keep(m1, bt)], axis=0)
        g_all = _dot_nt(x, ym)
        g1[cp] = jnp.where(strict_lower, g_all[:ch], 0.0)
        g2_s[i] = jnp.where(incl_lower, g_all[ch:], 0.0).astype(BF16)
        pw[cp] = jnp.where(m0, g1[cp][:, :LANES], g1[cp][:, LANES:])
        tinv[cp] = eye2 + pw[cp]
    yield

    def block_diag_of(cat):
        cat = cat.astype(BF16)
        return jnp.concatenate([keep(m0, cat), keep(m1, cat)], axis=0)

    n_sq = int(math.log2(ch)) - 1
    for cp in problems:
        pw[cp] = _dot(pw[cp].astype(BF16), block_diag_of(pw[cp]))
    yield
    for _ in range(n_sq - 1):
        for cp in problems:
            both = _dot(jnp.concatenate([tinv[cp], pw[cp]], axis=0).astype(BF16), block_diag_of(pw[cp]))
            tinv[cp] = tinv[cp] + both[:ch]
            pw[cp] = both[ch:]
        yield
    for cp in problems:
        tinv[cp] = tinv[cp] + _dot(tinv[cp].astype(BF16), block_diag_of(pw[cp]))
    yield
    for i, cp in enumerate(problems):
        vb = blk("vb", cp)
        v_stack = jnp.concatenate([zeros_cv, keep(m0, vb), keep(m1, vb), zeros_cv], axis=0)
        aakv_s[i] = _dot(g1[cp].astype(BF16), v_stack).astype(BF16)
        tcat_s[i] = tinv[cp].astype(BF16)
    yield
    for i, cp in enumerate(problems):
        j = zero_off + i
        rhs = jnp.concatenate([blk("at", cp), aakv_s[j]], axis=1)
        wu_s[i] = _dot(tcat_s[j], jnp.concatenate([keep(m0w, rhs), keep(m1w, rhs)], axis=0))
    yield
    for i, cp in enumerate(problems):
        j = zero_off + i
        upper = wu_s[j].astype(BF16)
        lower = jnp.concatenate([zeros_cv, blk("vb", cp)], axis=1)
        rhs = jnp.concatenate([keep(m0w, upper), keep(m0w, lower), keep(m1w, lower), keep(m1w, upper)], axis=0)
        ry_s[i] = _dot(g2_s[j], rhs)
    yield
    for i, cp in enumerate(problems):
        j = zero_off + i
        wu, ry = wu_s[j], ry_s[j]
        bkh = jnp.concatenate([blk("bh", cp), blk("kh", cp)], axis=0)
        rt2_s[i] = (blk("rt", cp).astype(F32) + ry[:, :LANES]).astype(BF16)
        y0_s[i] = ry[:, LANES:]
        wt_t = wu[:, :LANES].T.astype(BF16)
        mlr_s[i] = jnp.where(block_diag, _dot(wt_t, bkh[:ch]), 0.0).astype(BF16)
        uv_t = jnp.concatenate([wu[:, LANES:], blk("vb", cp).astype(F32)], axis=0).T.astype(BF16)
        n0_s[i] = jnp.where(block_diag, _dot(uv_t, bkh), 0.0)
    yield

    for c in range(n_chunks):
        rows = slice(c * ch, (c + 1) * ch)
        for p in range(n_pairs):
            cs = slice(p * LANES, (p + 1) * LANES)
            j = zero_off + (c * n_pairs + p)
            ht = state_ref[p]
            htb = ht.astype(BF16)
            y_ref[rows, cs] = _dot_nt(rt2_s[j], htb) + y0_s[j]
            state_ref[p] = ht * stage["egc"][c, 0:1, cs] + _dot(htb, mlr_s[j]) + n0_s[j]
    yield

    seg = seg_ref[...]
    y = y_ref[...]
    inv_hd = 1.0 / hd
    mu = _seg_sum(y, seg) * inv_hd
    yc = y - mu
    var = _seg_sum(yc * yc, seg) * inv_hd
    yn = yc * lax.rsqrt(var + GN_EPS) * gw_ref[...] + gb_ref[...]
    o_ref[...] = ((yn + stage["bonus"][...]) * stage["gate"][...]).astype(o_ref.dtype)


def _interleave(streams, order):
    for i in order:
        next(streams[i], None)
    for g in streams:
        for _ in g:
            pass


_RWKV_TRACE_ORDER = (0, 1, 0, 0, 1, 0, 0, 0, 0, 0, 1, 0, 1, 0, 1, 0, 1, 0, 0)


def _rwkv_kernel(*refs, lora_dims):
    n_in = 15
    (r_ref, k_ref, v_ref, xl_ref, wl_ref, w0_ref, a0_ref, kk_ref, ka_ref, rk_ref, gw_ref, gb_ref, seg_ref,
     tri_ref, zero_ref) = refs[:n_in]
    o_ref = refs[n_in]
    scratch = refs[n_in + 1:]
    state_ref, y_ref = scratch[:2]
    handoff = scratch[2:2 + _N_HANDOFF]
    ns = len(_STAGE_NAMES)
    stages = [dict(zip(_STAGE_NAMES, scratch[2 + _N_HANDOFF + i * ns: 2 + _N_HANDOFF + (i + 1) * ns]))
              for i in range(2)]
    t = pl.program_id(2)

    @pl.when(t == 0)
    def _():
        state_ref[...] = jnp.zeros_like(state_ref)
        for ref in stages[1].values():
            ref[...] = jnp.zeros_like(ref)

    for parity in range(2):
        @pl.when(t % 2 == parity)
        def _(parity=parity):
            _interleave([
                _rwkv_solve(stages[1 - parity], state_ref, y_ref, o_ref, gw_ref, gb_ref, seg_ref, handoff,
                            zero_ref[0]),
                _rwkv_prepare(r_ref, k_ref, v_ref, xl_ref, wl_ref, w0_ref, a0_ref, kk_ref, ka_ref, rk_ref,
                              seg_ref, tri_ref, stages[parity], lora_dims)], _RWKV_TRACE_ORDER)


def _rwkv7(zr, zl, w0, w_decay_up, a0, w_aaa_up, w_gate_up, k_k, k_a, r_k, gn_w, gn_b, batch, seq):
    tok = zr.shape[0]
    width = zr.shape[1] // 3
    lw = zl.shape[1]
    cw = RWKV_HPB * RWKV_HEAD_DIM
    nhb = width // cw
    tblk = RWKV_TBLK
    nt = seq // tblk
    ch = RWKV_CHUNK

    def pad_rows(w, start):
        return jnp.pad(w, ((start, lw - start - w.shape[0]), (0, 0)))

    wl = jnp.stack([pad_rows(w_decay_up, 0), pad_rows(w_aaa_up, DECAY_LORA),
                    pad_rows(w_gate_up, DECAY_LORA + AAA_LORA)], axis=0)
    wl = wl.reshape(3, lw, nhb, cw).transpose(2, 1, 0, 3).reshape(nhb, lw, 3 * cw).astype(BF16)

    idx = jnp.arange(MXU_DIM) // RWKV_HEAD_DIM
    seg = (idx[:, None] == idx[None, :]).astype(BF16)
    tri = (jnp.arange(ch)[:, None] >= jnp.arange(ch)[None, :]).astype(BF16)

    def in_blk(b, t):
        return b * nt + jnp.minimum(t, nt - 1)

    def col(sec):
        return pl.BlockSpec((tblk, cw), lambda b, h, t: (in_blk(b, t), sec * nhb + h))

    vec = pl.BlockSpec((1, cw), lambda b, h, t: (0, h))
    row_vec = lambda a: a.reshape(1, width)
    stage_shapes = ([pltpu.VMEM((tblk, cw), BF16)] * len(_STAGE_BF16)
                    + [pltpu.VMEM((tblk // ch, 8, cw), F32),
                       pltpu.VMEM((tblk, cw), F32), pltpu.VMEM((tblk, cw), F32)])
    n_prob = (tblk // ch) * (cw // LANES)
    handoff_shapes = [
        pltpu.VMEM((n_prob, ch, LANES), BF16),
        pltpu.VMEM((n_prob, ch, LANES), BF16),
        pltpu.VMEM((n_prob, ch, 2 * LANES), BF16),
        pltpu.VMEM((n_prob, ch, 2 * LANES), F32),
        pltpu.VMEM((n_prob, ch, 2 * LANES), F32),
        pltpu.VMEM((n_prob, ch, LANES), BF16), pltpu.VMEM((n_prob, ch, LANES), F32),
        pltpu.VMEM((n_prob, LANES, LANES), BF16), pltpu.VMEM((n_prob, LANES, LANES), F32),
    ]
    assert len(handoff_shapes) == _N_HANDOFF
    kern = functools.partial(_rwkv_kernel, lora_dims=(DECAY_LORA, AAA_LORA, GATE_LORA))
    return pl.pallas_call(
        kern,
        grid=(batch, nhb, nt + 1),
        in_specs=[
            col(0), col(1), col(2),
            pl.BlockSpec((tblk, lw), lambda b, h, t: (in_blk(b, t), 0)),
            pl.BlockSpec((None, lw, 3 * cw), lambda b, h, t: (h, 0, 0)),
            vec, vec, vec, vec, vec, vec, vec,
            pl.BlockSpec((MXU_DIM, MXU_DIM), lambda b, h, t: (0, 0)),
            pl.BlockSpec((ch, ch), lambda b, h, t: (0, 0)),
            pl.BlockSpec(memory_space=pltpu.SMEM),
        ],
        out_specs=pl.BlockSpec((tblk, cw), lambda b, h, t: (b * nt + jnp.maximum(t - 1, 0), h)),
        out_shape=jax.ShapeDtypeStruct((tok, width), BF16),
        scratch_shapes=[
            pltpu.VMEM((cw // LANES, LANES, LANES), F32),
            pltpu.VMEM((tblk, cw), F32),
        ] + handoff_shapes + stage_shapes * 2,
        compiler_params=_cparams(("parallel", "parallel", "arbitrary")),
        name="rwkv7_chunked",
    )(zr, zr, zr, zl, wl,
      row_vec(w0), row_vec(a0), row_vec(k_k), row_vec(k_a), row_vec(r_k), row_vec(gn_w), row_vec(gn_b),
      seg, tri, jnp.zeros((1,), jnp.int32))


def _merge_kernel(attn_ref, rw_ref, g0_ref, g1_ref, wa_ref, wr_ref, out_ref):
    attn_d = _dot(attn_ref[...], wa_ref[...].astype(BF16))
    rwkv_d = _dot(rw_ref[...], wr_ref[...].astype(BF16))
    out_ref[...] = (g0_ref[...].astype(F32) * attn_d + g1_ref[...].astype(F32) * rwkv_d).astype(out_ref.dtype)


def _merge(attn, rw, gates, w_attn_up, w_rwkv_up, tm=1024, tn=512):
    tok = rw.shape[0]
    d = w_attn_up.shape[1]
    nj = d // tn
    return pl.pallas_call(
        _merge_kernel,
        grid=(tok // tm, nj),
        in_specs=[
            pl.BlockSpec((tm, attn.shape[1]), lambda i, j: (i, 0)),
            pl.BlockSpec((tm, rw.shape[1]), lambda i, j: (i, 0)),
            pl.BlockSpec((tm, tn), lambda i, j: (i, j)),
            pl.BlockSpec((tm, tn), lambda i, j: (i, nj + j)),
            pl.BlockSpec((w_attn_up.shape[0], tn), lambda i, j: (0, j)),
            pl.BlockSpec((w_rwkv_up.shape[0], tn), lambda i, j: (0, j)),
        ],
        out_specs=pl.BlockSpec((tm, tn), lambda i, j: (i, j)),
        out_shape=jax.ShapeDtypeStruct((tok, d), BF16),
        compiler_params=_cparams(("parallel", "arbitrary")),
        name="merge_gated_up",
    )(attn, rw, gates, gates, w_attn_up, w_rwkv_up)


def _ple_kernel(h_ref, wg_ref, p_ref, wp_ref, x_ref, o_ref):
    gate = jax.nn.sigmoid(_dot(h_ref[...], wg_ref[...].astype(BF16)))
    o_ref[...] = x_ref[...] + gate * _dot(p_ref[...], wp_ref[...].astype(BF16))


def _ple(hn, w_gate, p, w_proj, x, tm=1024, tn=512):
    m, k = hn.shape
    n = w_gate.shape[1]
    kp = p.shape[1]
    return pl.pallas_call(
        _ple_kernel,
        grid=(m // tm, n // tn),
        in_specs=[
            pl.BlockSpec((tm, k), lambda i, j: (i, 0)),
            pl.BlockSpec((k, tn), lambda i, j: (0, j)),
            pl.BlockSpec((tm, kp), lambda i, j: (i, 0)),
            pl.BlockSpec((kp, tn), lambda i, j: (0, j)),
            pl.BlockSpec((tm, tn), lambda i, j: (i, j)),
        ],
        out_specs=pl.BlockSpec((tm, tn), lambda i, j: (i, j)),
        out_shape=jax.ShapeDtypeStruct((m, n), F32),
        compiler_params=_cparams(("parallel", "arbitrary")),
        name="ple_gated",
    )(hn, w_gate, p, w_proj, x)


def _layer(x2d, p2d, batch, seq, norm_mix, w_in, q_gain, k_gain, rel_bias, w_attn_up, shift_mix, w0,
           w_decay_up, a0, w_aaa_up, w_gate_up, k_k, k_a, r_k, gn_w, gn_b, w_rwkv_up, w_out, norm_mlp,
           w_mlp_in, w_mlp_out, norm_ple, w_ple_gate, w_ple_proj):
    n_attn_heads = len(DILATED_GROUPS) * ATTN_HEADS_PER_GROUP
    attn_width = n_attn_heads * ATTN_HEAD_DIM
    rwkv_width = w_rwkv_up.shape[0]
    a_end = 3 * attn_width
    z_end = a_end + 3 * rwkv_width
    r_end = z_end + DECAY_LORA + AAA_LORA + GATE_LORA

    h = _rmsnorm_bf16(x2d, norm_mix)

    gains = jnp.stack([q_gain * (ATTN_HEAD_DIM ** -0.5), k_gain, jnp.ones_like(q_gain)]).reshape(3, 1, LANES)
    w_in_t = jnp.swapaxes(w_in, 0, 1)
    tn = ATTN_HEADS_PER_GROUP * ATTN_HEAD_DIM
    qkv_groups = [_qkv_proj(h, w_in_t, gains, gi, dilation, batch, seq)
                  for gi, (_, dilation) in enumerate(DILATED_GROUPS)]
    mix_z = shift_mix[: z_end - a_end].reshape(1, -1)
    zr = _matmul_token_shift(h, w_in_t, mix_z, seq, n=z_end - a_end, wt_row_blk=a_end // tn, tn=tn,
                             name="w_in_rkv")
    mix_l = jnp.pad(shift_mix[z_end - a_end:], (0, tn - (r_end - z_end))).reshape(1, tn)
    zl = _matmul_token_shift(h, w_in_t, mix_l, seq, n=tn, wt_row_blk=z_end // tn, tn=tn, name="w_in_lora")
    gates = _matmul(h, w_in_t, n=w_in.shape[1] - r_end, wt_row0=r_end, act="sigmoid", out_dtype=BF16,
                    name="w_in_gates")

    attn = _dilated_attention(qkv_groups, rel_bias, batch, seq)

    rw = _rwkv7(zr, zl, w0, w_decay_up, a0, w_aaa_up, w_gate_up, k_k, k_a, r_k, gn_w, gn_b, batch, seq)

    merged = _merge(attn, rw, gates, w_attn_up, w_rwkv_up)
    x1 = _matmul_residual(merged, w_out, x2d, name="w_out_res")

    h2 = _rmsnorm_bf16(x1, norm_mlp)
    u = _matmul(h2, w_mlp_in, act="relu2", out_dtype=BF16, name="mlp_in")
    x2 = _matmul_kacc_residual(u, w_mlp_out, x1, name="mlp_out_res")

    hn = _rmsnorm_bf16(x2, norm_ple)
    return _ple(hn, w_ple_gate, p2d.astype(BF16), w_ple_proj, x2)


def kernel(x, p, norm_mix, w_in, q_gain, k_gain, rel_bias, w_attn_up, shift_mix, w0, w_decay_up, a0, w_aaa_up,
           w_gate_up, k_k, k_a, r_k, gn_w, gn_b, w_rwkv_up, w_out, norm_mlp, w_mlp_in, w_mlp_out, norm_ple,
           w_ple_gate, w_ple_proj):
    batch, seq, d_model = x.shape
    depth = p.shape[0]
    x2d = x.reshape(batch * seq, d_model)
    for i in range(depth):
        x2d = _layer(x2d, p[i].reshape(batch * seq, -1), batch, seq, norm_mix[i], w_in[i], q_gain[i], k_gain[i],
                     rel_bias, w_attn_up[i], shift_mix[i], w0[i], w_decay_up[i], a0[i], w_aaa_up[i],
                     w_gate_up[i], k_k[i], k_a[i], r_k[i], gn_w[i], gn_b[i], w_rwkv_up[i], w_out[i],
                     norm_mlp[i], w_mlp_in[i], w_mlp_out[i], norm_ple[i], w_ple_gate[i], w_ple_proj[i])
    return x2d.reshape(batch, seq, d_model)
```

```python
import functools
import math

import jax
import jax.numpy as jnp
from jax import lax
from jax.experimental import pallas as pl
from jax.experimental.pallas import tpu as pltpu

F32 = jnp.float32
BF16 = jnp.bfloat16

LANES = 128
SUBLANES = 8
MXU_DIM = 256
ATTN_HEAD_DIM = 128
ATTN_HEADS_PER_GROUP = 4
DILATED_GROUPS = ((128, 1), (512, 4), (2048, 16))
ATTN_BLOCK = 128
ATTN_UNITS_PER_STEP = 4
N_BUCKETS = 32
MAX_DISTANCE = 2048
RWKV_HEAD_DIM = 64
DECAY_LORA = 96
AAA_LORA = 96
GATE_LORA = 256
RMS_EPS = 1e-6
GN_EPS = 64e-5
MASK_VALUE = -1e30

RWKV_CHUNK = 64
RWKV_TBLK = 256
RWKV_HPB = 8

VMEM_LIMIT = 56 * 1024 * 1024


def _cparams(sem):
    return pltpu.CompilerParams(dimension_semantics=sem, vmem_limit_bytes=VMEM_LIMIT)


def _dot(a, b):
    return jnp.dot(a, b, preferred_element_type=F32)


def _dot_nt(a, b):
    return lax.dot_general(a, b, (((1,), (1,)), ((), ())), preferred_element_type=F32)


def _dot_split2(x, w_bf16):
    hi = x.astype(BF16)
    lo = (x - hi.astype(F32)).astype(BF16)
    return _dot(hi, w_bf16) + _dot(lo, w_bf16)


def _dot_split3_left(w_bf16, x):
    hi = x.astype(BF16)
    r1 = x - hi.astype(F32)
    mid = r1.astype(BF16)
    lo = (r1 - mid.astype(F32)).astype(BF16)
    return _dot(w_bf16, hi) + _dot(w_bf16, mid) + _dot(w_bf16, lo)


def _rmsnorm_kernel(x_ref, g_ref, o_ref):
    x = x_ref[...]
    ms = jnp.mean(x * x, axis=-1, keepdims=True)
    o_ref[...] = (x * lax.rsqrt(ms + RMS_EPS) * g_ref[...]).astype(o_ref.dtype)


def _rmsnorm_bf16(x, gain, tr=256):
    m, d = x.shape
    return pl.pallas_call(
        _rmsnorm_kernel,
        grid=(m // tr,),
        in_specs=[pl.BlockSpec((tr, d), lambda i: (i, 0)), pl.BlockSpec((1, d), lambda i: (0, 0))],
        out_specs=pl.BlockSpec((tr, d), lambda i: (i, 0)),
        out_shape=jax.ShapeDtypeStruct((m, d), BF16),
        compiler_params=_cparams(("parallel",)),
        name="rmsnorm",
    )(x, gain.reshape(1, d))


def _qkv_kernel(a_ref, w_ref, g_ref, o_ref, scr_ref, *, dilation):
    j = pl.program_id(1)
    acc = _dot_nt(a_ref[...], w_ref[...].astype(BF16))
    rows = acc.shape[0] // dilation
    for h in range(ATTN_HEADS_PER_GROUP):
        y = acc[:, h * LANES:(h + 1) * LANES]
        ms = jnp.mean(y * y, axis=-1, keepdims=True)
        y = y * jnp.where(j < 2, lax.rsqrt(ms + RMS_EPS), 1.0) * g_ref[...]
        if dilation == 1:
            o_ref[h] = y.astype(o_ref.dtype)
        else:
            scr_ref[h] = y
            for r in range(dilation):
                o_ref[h, :, r * LANES:(r + 1) * LANES] = (
                    scr_ref[h, pl.ds(r, rows, stride=dilation), :].astype(o_ref.dtype))


def _qkv_proj(h, w_in_t, gains, group, dilation, batch, seq, tm=1024):
    m, d = h.shape
    tn = ATTN_HEADS_PER_GROUP * LANES
    mt_per_b = seq // tm
    hg = ATTN_HEADS_PER_GROUP
    n_groups = len(DILATED_GROUPS)
    return pl.pallas_call(
        functools.partial(_qkv_kernel, dilation=dilation),
        grid=(m // tm, 3),
        in_specs=[
            pl.BlockSpec((tm, d), lambda i, j: (i, 0)),
            pl.BlockSpec((tn, d), lambda i, j: (j * n_groups + group, 0)),
            pl.BlockSpec((None, 1, LANES), lambda i, j: (j, 0, 0)),
        ],
        out_specs=pl.BlockSpec((None, hg, tm // dilation, dilation * LANES),
                               lambda i, j: (i // mt_per_b, j, i % mt_per_b, 0)),
        out_shape=jax.ShapeDtypeStruct((batch, 3 * hg, seq // dilation, dilation * LANES), BF16),
        scratch_shapes=[pltpu.VMEM((hg, tm, LANES), F32)],
        compiler_params=_cparams(("parallel", "arbitrary")),
        name=f"w_in_qkv_d{dilation}",
    )(h, w_in_t, gains)


def _mm_kernel(a_ref, w_ref, o_ref, *, act, transposed_w):
    w = w_ref[...].astype(BF16)
    acc = _dot_nt(a_ref[...], w) if transposed_w else _dot(a_ref[...], w)
    if act == "sigmoid":
        acc = jax.nn.sigmoid(acc)
    elif act == "relu2":
        acc = jnp.square(jnp.maximum(acc, 0.0))
    o_ref[...] = acc.astype(o_ref.dtype)


def _matmul(a, w, *, n=None, wt_row0=None, act=None, out_dtype=F32, tm=1024, tn=512, name="matmul"):
    m, k = a.shape
    transposed_w = wt_row0 is not None
    n = w.shape[1] if n is None else n
    if transposed_w:
        assert wt_row0 % SUBLANES == 0 and tn % SUBLANES == 0
        w_spec = pl.BlockSpec((pl.Element(tn), pl.Element(k)),
                              lambda i, j: (pl.multiple_of(wt_row0 + j * tn, SUBLANES), 0))
    else:
        w_spec = pl.BlockSpec((k, tn), lambda i, j: (0, j))
    return pl.pallas_call(
        functools.partial(_mm_kernel, act=act, transposed_w=transposed_w),
        grid=(m // tm, n // tn),
        in_specs=[pl.BlockSpec((tm, k), lambda i, j: (i, 0)), w_spec],
        out_specs=pl.BlockSpec((tm, tn), lambda i, j: (i, j)),
        out_shape=jax.ShapeDtypeStruct((m, n), out_dtype),
        compiler_params=_cparams(("parallel", "arbitrary")),
        name=name,
    )(a, w)


def _rms_normalise_rows(a_ref, g_ref, an_ref, rows_per_pass=128):
    def slab(s, carry):
        rows = pl.ds(pl.multiple_of(s * rows_per_pass, rows_per_pass), rows_per_pass)
        a = a_ref[rows, :].astype(F32)
        ms = jnp.mean(a * a, axis=-1, keepdims=True)
        an_ref[rows, :] = (a * lax.rsqrt(ms + RMS_EPS) * g_ref[...]).astype(an_ref.dtype)
        return carry

    lax.fori_loop(0, a_ref.shape[0] // rows_per_pass, slab, 0)


def _mm_norm_kernel(a_ref, g_ref, w_ref, o_ref, an_ref, *, act):
    @pl.when(pl.program_id(1) == 0)
    def _():
        _rms_normalise_rows(a_ref, g_ref, an_ref)

    acc = _dot(an_ref[...], w_ref[...].astype(BF16))
    if act == "relu2":
        acc = jnp.square(jnp.maximum(acc, 0.0))
    o_ref[...] = acc.astype(o_ref.dtype)


def _matmul_rmsnorm(a, gain, w, *, act=None, out_dtype=F32, tm=1024, tn=1024, name="matmul_norm"):
    m, k = a.shape
    n = w.shape[1]
    return pl.pallas_call(
        functools.partial(_mm_norm_kernel, act=act),
        grid=(m // tm, n // tn),
        in_specs=[pl.BlockSpec((tm, k), lambda i, j: (i, 0)),
                  pl.BlockSpec((1, k), lambda i, j: (0, 0)),
                  pl.BlockSpec((k, tn), lambda i, j: (0, j))],
        out_specs=pl.BlockSpec((tm, tn), lambda i, j: (i, j)),
        out_shape=jax.ShapeDtypeStruct((m, n), out_dtype),
        scratch_shapes=[pltpu.VMEM((tm, k), BF16)],
        compiler_params=_cparams(("parallel", "arbitrary")),
        name=name,
    )(a, gain.reshape(1, k), w)


def _mm_shift_kernel(a_ref, w_ref, mix_ref, o_ref, carry_ref, *, m_tiles_per_seq):
    i = pl.program_id(0)
    j = pl.program_id(1)

    @pl.when(i == 0)
    def _():
        carry_ref[j] = jnp.zeros(carry_ref.shape[1:], F32)

    z = _dot_nt(a_ref[...], w_ref[...].astype(BF16))
    tm = z.shape[0]
    row = lax.broadcasted_iota(jnp.int32, (tm, 1), 0)
    sublanes = carry_ref.shape[1]
    prev_row = jnp.where(i % m_tiles_per_seq == 0, 0.0, carry_ref[j, sublanes - 1:sublanes, :])
    prev = jnp.where(row == 0, prev_row, pltpu.roll(z, 1, axis=0))
    o_ref[...] = z + mix_ref[...] * (prev - z)
    carry_ref[j] = z[tm - sublanes:tm, :]


def _matmul_token_shift(a, wt, mix, seq, *, n, wt_row_blk, tm=1024, tn=512, name="matmul_shift"):
    m, k = a.shape
    n_tiles = n // tn
    return pl.pallas_call(
        functools.partial(_mm_shift_kernel, m_tiles_per_seq=seq // tm),
        grid=(m // tm, n_tiles),
        in_specs=[pl.BlockSpec((tm, k), lambda i, j: (i, 0)),
                  pl.BlockSpec((tn, k), lambda i, j: (wt_row_blk + j, 0)),
                  pl.BlockSpec((1, tn), lambda i, j: (0, j))],
        out_specs=pl.BlockSpec((tm, tn), lambda i, j: (i, j)),
        out_shape=jax.ShapeDtypeStruct((m, n), F32),
        scratch_shapes=[pltpu.VMEM((n_tiles, SUBLANES, tn), F32)],
        compiler_params=_cparams(("arbitrary", "arbitrary")),
        name=name,
    )(a, wt, mix)


def _mm_res_kernel(a_ref, w_ref, r_ref, o_ref, ob_ref):
    out = r_ref[...] + _dot(a_ref[...], w_ref[...].astype(BF16))
    o_ref[...] = out
    ob_ref[...] = out.astype(ob_ref.dtype)


def _matmul_residual(a, w, res, *, tm=1024, tn=512, name="matmul_res"):
    m, k = a.shape
    n = w.shape[1]
    return pl.pallas_call(
        _mm_res_kernel,
        grid=(m // tm, n // tn),
        in_specs=[
            pl.BlockSpec((tm, k), lambda i, j: (i, 0)),
            pl.BlockSpec((k, tn), lambda i, j: (0, j)),
            pl.BlockSpec((tm, tn), lambda i, j: (i, j)),
        ],
        out_specs=[pl.BlockSpec((tm, tn), lambda i, j: (i, j))] * 2,
        out_shape=[jax.ShapeDtypeStruct((m, n), F32), jax.ShapeDtypeStruct((m, n), BF16)],
        compiler_params=_cparams(("parallel", "arbitrary")),
        name=name,
    )(a, w, res)


def _mm_kacc_res_kernel(a_ref, w_ref, r_ref, o_ref, ob_ref):
    k = pl.program_id(2)

    @pl.when(k == 0)
    def _():
        o_ref[...] = r_ref[...] + _dot(a_ref[...], w_ref[...].astype(BF16))

    @pl.when(k > 0)
    def _():
        o_ref[...] += _dot(a_ref[...], w_ref[...].astype(BF16))

    @pl.when(k == pl.num_programs(2) - 1)
    def _():
        ob_ref[...] = o_ref[...].astype(ob_ref.dtype)


def _matmul_kacc_residual(a, w, res, *, tm=1024, tn=1024, tk=2048, name="matmul_kacc"):
    m, kdim = a.shape
    n = w.shape[1]
    return pl.pallas_call(
        _mm_kacc_res_kernel,
        grid=(m // tm, n // tn, kdim // tk),
        in_specs=[
            pl.BlockSpec((tm, tk), lambda i, j, k: (i, k)),
            pl.BlockSpec((tk, tn), lambda i, j, k: (k, j)),
            pl.BlockSpec((tm, tn), lambda i, j, k: (i, j)),
        ],
        out_specs=[pl.BlockSpec((tm, tn), lambda i, j, k: (i, j))] * 2,
        out_shape=[jax.ShapeDtypeStruct((m, n), F32), jax.ShapeDtypeStruct((m, n), BF16)],
        compiler_params=_cparams(("parallel", "parallel", "arbitrary")),
        name=name,
    )(a, w, res)


def _t5_bucket(dist):
    max_exact = N_BUCKETS // 2
    d_f = jnp.maximum(dist, 1).astype(F32)
    large = max_exact + (jnp.log(d_f / max_exact) / math.log(MAX_DISTANCE / max_exact)
                         * (N_BUCKETS - max_exact)).astype(jnp.int32)
    large = jnp.minimum(large, N_BUCKETS - 1)
    return jnp.where(dist < max_exact, dist, large)


def _band_buckets():
    blk = ATTN_BLOCK
    rel = (blk + jnp.arange(blk))[:, None] - jnp.arange(2 * blk)[None, :]
    out = []
    for window, dilation in DILATED_GROUPS:
        band = (rel >= 0) & (rel <= window // dilation)
        out.append(jnp.where(band, _t5_bucket(jnp.maximum(rel, 0) * dilation), -1))
    return jnp.stack(out).astype(jnp.int32)


def _attn_kernel(q0, k0, v0, q1, k1, v1, q2, k2, v2, bucket_ref, tab_ref, o_ref, oacc_ref, lacc_ref):
    blk = ATTN_BLOCK
    hh = pl.program_id(1)
    seq = oacc_ref.shape[0]
    refs = ((q0, k0, v0), (q1, k1, v1), (q2, k2, v2))

    def softmax_out(s, v):
        m = jnp.max(s, axis=-1, keepdims=True)
        e = jnp.exp(s - m)
        l = jnp.sum(e, axis=-1, keepdims=True)
        o = _dot(e.astype(BF16), v) / l
        return o, jnp.broadcast_to(m + jnp.log(l), (blk, LANES))

    for g, (_, dilation) in enumerate(DILATED_GROUPS):
        q_ref, k_ref, v_ref = refs[g]
        n_blocks = seq // (dilation * blk)
        bucket = bucket_ref[g]
        head = g * ATTN_HEADS_PER_GROUP + hh
        bias = jnp.where(bucket < 0, MASK_VALUE, 0.0)
        for b in range(N_BUCKETS):
            bias = jnp.where(bucket == b, tab_ref[b, head], bias)

        def merge_store(tok_rows, o, lse, g=g):
            if g == 0:
                oacc_ref[tok_rows, :] = o
                lacc_ref[tok_rows, :] = lse
            else:
                o_old = oacc_ref[tok_rows, :]
                l_old = lacc_ref[tok_rows, :]
                m = jnp.maximum(l_old, lse)
                w_old = jnp.exp(l_old - m)
                w_new = jnp.exp(lse - m)
                den = w_old + w_new
                oacc_ref[tok_rows, :] = (w_old * o_old + w_new * o) / den
                lacc_ref[tok_rows, :] = m + jnp.log(den)

        def tok_rows(off, r, dilation=dilation):
            if dilation == 1:
                return pl.ds(off, blk)
            return pl.ds(off * dilation + r, blk, stride=dilation)

        def unit(off, r, q_ref=q_ref, k_ref=k_ref, v_ref=v_ref, bias=bias, merge_store=merge_store,
                 tok_rows=tok_rows):
            cs = slice(r * LANES, (r + 1) * LANES)
            s = _dot_nt(q_ref[pl.ds(off, blk), cs], k_ref[pl.ds(off - blk, 2 * blk), cs]) + bias
            o, lse = softmax_out(s, v_ref[pl.ds(off - blk, 2 * blk), cs])
            merge_store(tok_rows(off, r), o, lse)

        for r in range(dilation):
            cs = slice(r * LANES, (r + 1) * LANES)
            s0 = _dot_nt(q_ref[0:blk, cs], k_ref[0:blk, cs]) + bias[:, blk:]
            o0, lse0 = softmax_out(s0, v_ref[0:blk, cs])
            merge_store(tok_rows(0, r), o0, lse0)

        ways = ATTN_UNITS_PER_STEP
        if dilation == 1:
            part = n_blocks // ways
            for w in range(1, ways):
                unit(w * part * blk, 0)

            def body(n, carry, unit=unit, part=part):
                off = pl.multiple_of(n * blk, blk)
                for w in range(ways):
                    unit(off + w * part * blk, 0)
                return carry

            lax.fori_loop(1, part, body, 0)
        else:
            for r in range(0, dilation, ways):
                def body(n, carry, unit=unit, r=r):
                    off = pl.multiple_of(n * blk, blk)
                    for w in range(ways):
                        unit(off, r + w)
                    return carry

                lax.fori_loop(1, n_blocks, body, 0)

    o_ref[...] = oacc_ref[...].astype(o_ref.dtype)


def _dilated_attention(qkv_groups, rel_bias, batch, seq):
    hg = ATTN_HEADS_PER_GROUP
    in_specs, operands = [], []
    for (_, dilation), arr in zip(DILATED_GROUPS, qkv_groups):
        for section in range(3):
            in_specs.append(pl.BlockSpec((None, None, seq // dilation, dilation * LANES),
                                         lambda b, h, section=section: (b, section * hg + h, 0, 0)))
            operands.append(arr)
    buckets = _band_buckets()
    in_specs.append(pl.BlockSpec(buckets.shape, lambda b, h: (0, 0, 0)))
    in_specs.append(pl.BlockSpec(memory_space=pltpu.SMEM))
    return pl.pallas_call(
        _attn_kernel,
        grid=(batch, hg),
        in_specs=in_specs,
        out_specs=pl.BlockSpec((seq, LANES), lambda b, h: (b, h)),
        out_shape=jax.ShapeDtypeStruct((batch * seq, hg * LANES), BF16),
        scratch_shapes=[pltpu.VMEM((seq, LANES), F32), pltpu.VMEM((seq, LANES), F32)],
        compiler_params=_cparams(("parallel", "parallel")),
        name="dilated_attn",
    )(*operands, buckets, rel_bias.astype(F32))


_STAGE_BF16 = ("at", "rt", "bt", "kt", "bh", "kh", "vb")
_STAGE_F32 = ("egc", "bonus", "gate")
_STAGE_NAMES = _STAGE_BF16 + _STAGE_F32
_N_HANDOFF = 9


def _seg_sum(x, seg):
    w = seg.shape[0]
    return jnp.concatenate([_dot_split2(x[:, i * w:(i + 1) * w], seg) for i in range(x.shape[1] // w)], axis=1)


def _rwkv_prepare(r_ref, k_ref, v_ref, xl_ref, wl_ref, w0_ref, a0_ref, kk_ref, ka_ref, rk_ref, seg_ref, tri_ref,
                  stage, lora_dims):
    tblk, cw = r_ref.shape
    ch = RWKV_CHUNK
    r, k, v, xl = r_ref[...], k_ref[...], v_ref[...], xl_ref[...]

    d_w, d_a, _ = lora_dims
    lane_l = lax.broadcasted_iota(jnp.int32, xl.shape, 1)
    act = jnp.where(lane_l < d_w, jnp.tanh(xl), jnp.where(lane_l < d_w + d_a, xl, jax.nn.sigmoid(xl)))
    up = _dot(act.astype(BF16), wl_ref[...])
    w_arg = -(w0_ref[...] + up[:, 0:cw])
    softplus = jnp.maximum(w_arg, 0.0) + jnp.log1p(jnp.exp(-jnp.abs(w_arg)))
    logdecay = -jnp.exp(-softplus - 0.5)
    a_gate = jax.nn.sigmoid(a0_ref[...] + up[:, cw:2 * cw])
    stage["gate"][...] = up[:, 2 * cw:3 * cw]
    yield

    seg = seg_ref[...]
    kk = k * kk_ref[...]
    kk = kk * lax.rsqrt(jnp.maximum(_seg_sum(kk * kk, seg), 1e-24))
    k2 = k * (1.0 + (a_gate - 1.0) * ka_ref[...])
    a_vec = -kk
    b_vec = kk * a_gate
    stage["bonus"][...] = _seg_sum(r * k2 * rk_ref[...], seg) * v
    yield

    tri = tri_ref[...]
    for c in range(tblk // ch):
        rows = slice(c * ch, (c + 1) * ch)
        ld = logdecay[rows]
        g_inc = _dot_split3_left(tri, ld)
        g_exc = g_inc - ld
        e_neg = jnp.exp(-g_inc)
        e_last = jnp.exp(g_inc[ch - 1:ch, :])
        bt = b_vec[rows] * e_neg
        kt = k2[rows] * e_neg
        stage["at"][rows, :] = (a_vec[rows] * jnp.exp(g_exc)).astype(BF16)
        stage["rt"][rows, :] = (r[rows] * jnp.exp(g_inc)).astype(BF16)
        stage["bt"][rows, :] = bt.astype(BF16)
        stage["kt"][rows, :] = kt.astype(BF16)
        stage["bh"][rows, :] = (bt * e_last).astype(BF16)
        stage["kh"][rows, :] = (kt * e_last).astype(BF16)
        stage["egc"][c] = jnp.broadcast_to(e_last, stage["egc"].shape[1:])
        yield
    stage["vb"][...] = v.astype(BF16)


def _rwkv_solve(stage, state_ref, y_ref, o_ref, gw_ref, gb_ref, seg_ref, handoff, zero_off):
    tblk, cw = o_ref.shape
    ch = RWKV_CHUNK
    n_chunks = tblk // ch
    n_pairs = cw // LANES
    hd = RWKV_HEAD_DIM

    lane = lax.broadcasted_iota(jnp.int32, (1, LANES), 1)
    m0, m1 = lane < hd, lane >= hd
    lane2 = lax.broadcasted_iota(jnp.int32, (1, 2 * LANES), 1) % LANES
    m0w, m1w = lane2 < hd, lane2 >= hd
    rr = lax.broadcasted_iota(jnp.int32, (ch, 2 * LANES), 0)
    cc = lax.broadcasted_iota(jnp.int32, (ch, 2 * LANES), 1) % ch
    strict_lower = cc < rr
    incl_lower = cc <= rr
    eye2 = (lax.broadcasted_iota(jnp.int32, (ch, LANES), 0)
            == lax.broadcasted_iota(jnp.int32, (ch, LANES), 1) % ch).astype(F32)
    bd_r = lax.broadcasted_iota(jnp.int32, (LANES, LANES), 0) >= hd
    bd_c = lax.broadcasted_iota(jnp.int32, (LANES, LANES), 1) >= hd
    block_diag = bd_r == bd_c
    zeros_cv = jnp.zeros((ch, LANES), BF16)

    def keep(mask, x):
        return jnp.where(mask, x, jnp.zeros_like(x))

    problems = [(c, p) for c in range(n_chunks) for p in range(n_pairs)]

    def blk(name, cp):
        c, p = cp
        return stage[name][c * ch:(c + 1) * ch, p * LANES:(p + 1) * LANES]

    aakv_s, tcat_s, g2_s, wu_s, ry_s, rt2_s, y0_s, mlr_s, n0_s = handoff

    g1, pw, tinv = {}, {}, {}
    for i, cp in enumerate(problems):
        bt, kt = blk("bt", cp), blk("kt", cp)
        x = jnp.concatenate([blk("at", cp), blk("rt", cp)], axis=0)
        ym = jnp.concatenate([keep(m0, bt), keep(m0, kt), keep(m1, kt), keep(m1, bt)], axis=0)
        g_all = _dot_nt(x, ym)
        g1[cp] = jnp.where(strict_lower, g_all[:ch], 0.0)
        g2_s[i] = jnp.where(incl_lower, g_all[ch:], 0.0).astype(BF16)
        pw[cp] = jnp.where(m0, g1[cp][:, :LANES], g1[cp][:, LANES:])
        tinv[cp] = eye2 + pw[cp]
    yield

    def block_diag_of(cat):
        cat = cat.astype(BF16)
        return jnp.concatenate([keep(m0, cat), keep(m1, cat)], axis=0)

    n_sq = int(math.log2(ch)) - 1
    for cp in problems:
        pw[cp] = _dot(pw[cp].astype(BF16), block_diag_of(pw[cp]))
    yield
    for _ in range(n_sq - 1):
        for cp in problems:
            both = _dot(jnp.concatenate([tinv[cp], pw[cp]], axis=0).astype(BF16), block_diag_of(pw[cp]))
            tinv[cp] = tinv[cp] + both[:ch]
            pw[cp] = both[ch:]
        yield
    for cp in problems:
        tinv[cp] = tinv[cp] + _dot(tinv[cp].astype(BF16), block_diag_of(pw[cp]))
    yield
    for i, cp in enumerate(problems):
        vb = blk("vb", cp)
        v_stack = jnp.concatenate([zeros_cv, keep(m0, vb), keep(m1, vb), zeros_cv], axis=0)
        aakv_s[i] = _dot(g1[cp].astype(BF16), v_stack).astype(BF16)
        tcat_s[i] = tinv[cp].astype(BF16)
    yield
    for i, cp in enumerate(problems):
        j = zero_off + i
        rhs = jnp.concatenate([blk("at", cp), aakv_s[j]], axis=1)
        wu_s[i] = _dot(tcat_s[j], jnp.concatenate([keep(m0w, rhs), keep(m1w, rhs)], axis=0))
    yield
    for i, cp in enumerate(problems):
        j = zero_off + i
        upper = wu_s[j].astype(BF16)
        lower = jnp.concatenate([zeros_cv, blk("vb", cp)], axis=1)
        rhs = jnp.concatenate([keep(m0w, upper), keep(m0w, lower), keep(m1w, lower), keep(m1w, upper)], axis=0)
        ry_s[i] = _dot(g2_s[j], rhs)
    yield
    for i, cp in enumerate(problems):
        j = zero_off + i
        wu, ry = wu_s[j], ry_s[j]
        bkh = jnp.concatenate([blk("bh", cp), blk("kh", cp)], axis=0)
        rt2_s[i] = (blk("rt", cp).astype(F32) + ry[:, :LANES]).astype(BF16)
        y0_s[i] = ry[:, LANES:]
        wt_t = wu[:, :LANES].T.astype(BF16)
        mlr_s[i] = jnp.where(block_diag, _dot(wt_t, bkh[:ch]), 0.0).astype(BF16)
        uv_t = jnp.concatenate([wu[:, LANES:], blk("vb", cp).astype(F32)], axis=0).T.astype(BF16)
        n0_s[i] = jnp.where(block_diag, _dot(uv_t, bkh), 0.0)
    yield

    for c in range(n_chunks):
        rows = slice(c * ch, (c + 1) * ch)
        for p in range(n_pairs):
            cs = slice(p * LANES, (p + 1) * LANES)
            j = zero_off + (c * n_pairs + p)
            ht = state_ref[p]
            htb = ht.astype(BF16)
            y_ref[rows, cs] = _dot_nt(rt2_s[j], htb) + y0_s[j]
            state_ref[p] = ht * stage["egc"][c, 0:1, cs] + _dot(htb, mlr_s[j]) + n0_s[j]
    yield

    seg = seg_ref[...]
    y = y_ref[...]
    inv_hd = 1.0 / hd
    mu = _seg_sum(y, seg) * inv_hd
    yc = y - mu
    var = _seg_sum(yc * yc, seg) * inv_hd
    yn = yc * lax.rsqrt(var + GN_EPS) * gw_ref[...] + gb_ref[...]
    o_ref[...] = ((yn + stage["bonus"][...]) * stage["gate"][...]).astype(o_ref.dtype)


def _interleave(streams, order):
    for i in order:
        next(streams[i], None)
    for g in streams:
        for _ in g:
            pass


_RWKV_TRACE_ORDER = (0, 1, 0, 0, 1, 0, 0, 0, 0, 0, 1, 0, 1, 0, 1, 0, 1, 0, 0)


def _rwkv_kernel(*refs, lora_dims):
    n_in = 15
    (r_ref, k_ref, v_ref, xl_ref, wl_ref, w0_ref, a0_ref, kk_ref, ka_ref, rk_ref, gw_ref, gb_ref, seg_ref,
     tri_ref, zero_ref) = refs[:n_in]
    o_ref = refs[n_in]
    scratch = refs[n_in + 1:]
    state_ref, y_ref = scratch[:2]
    handoff = scratch[2:2 + _N_HANDOFF]
    ns = len(_STAGE_NAMES)
    stages = [dict(zip(_STAGE_NAMES, scratch[2 + _N_HANDOFF + i * ns: 2 + _N_HANDOFF + (i + 1) * ns]))
              for i in range(2)]
    t = pl.program_id(2)

    @pl.when(t == 0)
    def _():
        state_ref[...] = jnp.zeros_like(state_ref)
        for ref in stages[1].values():
            ref[...] = jnp.zeros_like(ref)

    for parity in range(2):
        @pl.when(t % 2 == parity)
        def _(parity=parity):
            _interleave([
                _rwkv_solve(stages[1 - parity], state_ref, y_ref, o_ref, gw_ref, gb_ref, seg_ref, handoff,
                            zero_ref[0]),
                _rwkv_prepare(r_ref, k_ref, v_ref, xl_ref, wl_ref, w0_ref, a0_ref, kk_ref, ka_ref, rk_ref,
                              seg_ref, tri_ref, stages[parity], lora_dims)], _RWKV_TRACE_ORDER)


def _rwkv7(zr, zl, w0, w_decay_up, a0, w_aaa_up, w_gate_up, k_k, k_a, r_k, gn_w, gn_b, batch, seq):
    tok = zr.shape[0]
    width = zr.shape[1] // 3
    lw = zl.shape[1]
    cw = RWKV_HPB * RWKV_HEAD_DIM
    nhb = width // cw
    tblk = RWKV_TBLK
    nt = seq // tblk
    ch = RWKV_CHUNK

    def pad_rows(w, start):
        return jnp.pad(w, ((start, lw - start - w.shape[0]), (0, 0)))

    wl = jnp.stack([pad_rows(w_decay_up, 0), pad_rows(w_aaa_up, DECAY_LORA),
                    pad_rows(w_gate_up, DECAY_LORA + AAA_LORA)], axis=0)
    wl = wl.reshape(3, lw, nhb, cw).transpose(2, 1, 0, 3).reshape(nhb, lw, 3 * cw).astype(BF16)

    idx = jnp.arange(MXU_DIM) // RWKV_HEAD_DIM
    seg = (idx[:, None] == idx[None, :]).astype(BF16)
    tri = (jnp.arange(ch)[:, None] >= jnp.arange(ch)[None, :]).astype(BF16)

    def in_blk(b, t):
        return b * nt + jnp.minimum(t, nt - 1)

    def col(sec):
        return pl.BlockSpec((tblk, cw), lambda b, h, t: (in_blk(b, t), sec * nhb + h))

    vec = pl.BlockSpec((1, cw), lambda b, h, t: (0, h))
    row_vec = lambda a: a.reshape(1, width)
    stage_shapes = ([pltpu.VMEM((tblk, cw), BF16)] * len(_STAGE_BF16)
                    + [pltpu.VMEM((tblk // ch, 8, cw), F32),
                       pltpu.VMEM((tblk, cw), F32), pltpu.VMEM((tblk, cw), F32)])
    n_prob = (tblk // ch) * (cw // LANES)
    handoff_shapes = [
        pltpu.VMEM((n_prob, ch, LANES), BF16),
        pltpu.VMEM((n_prob, ch, LANES), BF16),
        pltpu.VMEM((n_prob, ch, 2 * LANES), BF16),
        pltpu.VMEM((n_prob, ch, 2 * LANES), F32),
        pltpu.VMEM((n_prob, ch, 2 * LANES), F32),
        pltpu.VMEM((n_prob, ch, LANES), BF16), pltpu.VMEM((n_prob, ch, LANES), F32),
        pltpu.VMEM((n_prob, LANES, LANES), BF16), pltpu.VMEM((n_prob, LANES, LANES), F32),
    ]
    assert len(handoff_shapes) == _N_HANDOFF
    kern = functools.partial(_rwkv_kernel, lora_dims=(DECAY_LORA, AAA_LORA, GATE_LORA))
    return pl.pallas_call(
        kern,
        grid=(batch, nhb, nt + 1),
        in_specs=[
            col(0), col(1), col(2),
            pl.BlockSpec((tblk, lw), lambda b, h, t: (in_blk(b, t), 0)),
            pl.BlockSpec((None, lw, 3 * cw), lambda b, h, t: (h, 0, 0)),
            vec, vec, vec, vec, vec, vec, vec,
            pl.BlockSpec((MXU_DIM, MXU_DIM), lambda b, h, t: (0, 0)),
            pl.BlockSpec((ch, ch), lambda b, h, t: (0, 0)),
            pl.BlockSpec(memory_space=pltpu.SMEM),
        ],
        out_specs=pl.BlockSpec((tblk, cw), lambda b, h, t: (b * nt + jnp.maximum(t - 1, 0), h)),
        out_shape=jax.ShapeDtypeStruct((tok, width), BF16),
        scratch_shapes=[
            pltpu.VMEM((cw // LANES, LANES, LANES), F32),
            pltpu.VMEM((tblk, cw), F32),
        ] + handoff_shapes + stage_shapes * 2,
        compiler_params=_cparams(("parallel", "parallel", "arbitrary")),
        name="rwkv7_chunked",
    )(zr, zr, zr, zl, wl,
      row_vec(w0), row_vec(a0), row_vec(k_k), row_vec(k_a), row_vec(r_k), row_vec(gn_w), row_vec(gn_b),
      seg, tri, jnp.zeros((1,), jnp.int32))


def _merge_kernel(attn_ref, rw_ref, g0_ref, g1_ref, wa_ref, wr_ref, out_ref):
    attn_d = _dot(attn_ref[...], wa_ref[...].astype(BF16))
    rwkv_d = _dot(rw_ref[...], wr_ref[...].astype(BF16))
    out_ref[...] = (g0_ref[...].astype(F32) * attn_d + g1_ref[...].astype(F32) * rwkv_d).astype(out_ref.dtype)


def _merge(attn, rw, gates, w_attn_up, w_rwkv_up, tm=1024, tn=512):
    tok = rw.shape[0]
    d = w_attn_up.shape[1]
    nj = d // tn
    return pl.pallas_call(
        _merge_kernel,
        grid=(tok // tm, nj),
        in_specs=[
            pl.BlockSpec((tm, attn.shape[1]), lambda i, j: (i, 0)),
            pl.BlockSpec((tm, rw.shape[1]), lambda i, j: (i, 0)),
            pl.BlockSpec((tm, tn), lambda i, j: (i, j)),
            pl.BlockSpec((tm, tn), lambda i, j: (i, nj + j)),
            pl.BlockSpec((w_attn_up.shape[0], tn), lambda i, j: (0, j)),
            pl.BlockSpec((w_rwkv_up.shape[0], tn), lambda i, j: (0, j)),
        ],
        out_specs=pl.BlockSpec((tm, tn), lambda i, j: (i, j)),
        out_shape=jax.ShapeDtypeStruct((tok, d), BF16),
        compiler_params=_cparams(("parallel", "arbitrary")),
        name="merge_gated_up",
    )(attn, rw, gates, gates, w_attn_up, w_rwkv_up)


def _ple_kernel(xb_ref, g_ref, wg_ref, p_ref, wp_ref, x_ref, o_ref, an_ref):
    @pl.when(pl.program_id(1) == 0)
    def _():
        _rms_normalise_rows(xb_ref, g_ref, an_ref)

    gate = jax.nn.sigmoid(_dot(an_ref[...], wg_ref[...].astype(BF16)))
    o_ref[...] = x_ref[...] + gate * _dot(p_ref[...], wp_ref[...].astype(BF16))


def _ple(xb, gain, w_gate, p, w_proj, x, tm=1024, tn=512):
    m, k = xb.shape
    n = w_gate.shape[1]
    kp = p.shape[1]
    return pl.pallas_call(
        _ple_kernel,
        grid=(m // tm, n // tn),
        in_specs=[
            pl.BlockSpec((tm, k), lambda i, j: (i, 0)),
            pl.BlockSpec((1, k), lambda i, j: (0, 0)),
            pl.BlockSpec((k, tn), lambda i, j: (0, j)),
            pl.BlockSpec((tm, kp), lambda i, j: (i, 0)),
            pl.BlockSpec((kp, tn), lambda i, j: (0, j)),
            pl.BlockSpec((tm, tn), lambda i, j: (i, j)),
        ],
        out_specs=pl.BlockSpec((tm, tn), lambda i, j: (i, j)),
        out_shape=jax.ShapeDtypeStruct((m, n), F32),
        scratch_shapes=[pltpu.VMEM((tm, k), BF16)],
        compiler_params=_cparams(("parallel", "arbitrary")),
        name="ple_gated",
    )(xb, gain.reshape(1, k), w_gate, p, w_proj, x)


def _layer(x2d, p2d, batch, seq, norm_mix, w_in, q_gain, k_gain, rel_bias, w_attn_up, shift_mix, w0,
           w_decay_up, a0, w_aaa_up, w_gate_up, k_k, k_a, r_k, gn_w, gn_b, w_rwkv_up, w_out, norm_mlp,
           w_mlp_in, w_mlp_out, norm_ple, w_ple_gate, w_ple_proj):
    n_attn_heads = len(DILATED_GROUPS) * ATTN_HEADS_PER_GROUP
    attn_width = n_attn_heads * ATTN_HEAD_DIM
    rwkv_width = w_rwkv_up.shape[0]
    a_end = 3 * attn_width
    z_end = a_end + 3 * rwkv_width
    r_end = z_end + DECAY_LORA + AAA_LORA + GATE_LORA

    h = _rmsnorm_bf16(x2d, norm_mix)

    gains = jnp.stack([q_gain * (ATTN_HEAD_DIM ** -0.5), k_gain, jnp.ones_like(q_gain)]).reshape(3, 1, LANES)
    w_in_t = jnp.swapaxes(w_in, 0, 1)
    tn = ATTN_HEADS_PER_GROUP * ATTN_HEAD_DIM
    qkv_groups = [_qkv_proj(h, w_in_t, gains, gi, dilation, batch, seq)
                  for gi, (_, dilation) in enumerate(DILATED_GROUPS)]
    mix_z = shift_mix[: z_end - a_end].reshape(1, -1)
    zr = _matmul_token_shift(h, w_in_t, mix_z, seq, n=z_end - a_end, wt_row_blk=a_end // tn, tn=tn,
                             name="w_in_rkv")
    mix_l = jnp.pad(shift_mix[z_end - a_end:], (0, tn - (r_end - z_end))).reshape(1, tn)
    zl = _matmul_token_shift(h, w_in_t, mix_l, seq, n=tn, wt_row_blk=z_end // tn, tn=tn, name="w_in_lora")
    gates = _matmul(h, w_in_t, n=w_in.shape[1] - r_end, wt_row0=r_end, act="sigmoid", out_dtype=BF16,
                    name="w_in_gates")

    attn = _dilated_attention(qkv_groups, rel_bias, batch, seq)

    rw = _rwkv7(zr, zl, w0, w_decay_up, a0, w_aaa_up, w_gate_up, k_k, k_a, r_k, gn_w, gn_b, batch, seq)

    merged = _merge(attn, rw, gates, w_attn_up, w_rwkv_up)
    x1, x1b = _matmul_residual(merged, w_out.astype(BF16), x2d, name="w_out_res")

    u = _matmul_rmsnorm(x1b, norm_mlp, w_mlp_in.astype(BF16), act="relu2", out_dtype=BF16, name="mlp_in")
    x2, x2b = _matmul_kacc_residual(u, w_mlp_out, x1, name="mlp_out_res")

    return _ple(x2b, norm_ple, w_ple_gate.astype(BF16), p2d.astype(BF16), w_ple_proj, x2)


def kernel(x, p, norm_mix, w_in, q_gain, k_gain, rel_bias, w_attn_up, shift_mix, w0, w_decay_up, a0, w_aaa_up,
           w_gate_up, k_k, k_a, r_k, gn_w, gn_b, w_rwkv_up, w_out, norm_mlp, w_mlp_in, w_mlp_out, norm_ple,
           w_ple_gate, w_ple_proj):
    batch, seq, d_model = x.shape
    depth = p.shape[0]
    x2d = x.reshape(batch * seq, d_model)
    for i in range(depth):
        x2d = _layer(x2d, p[i].reshape(batch * seq, -1), batch, seq, norm_mix[i], w_in[i], q_gain[i], k_gain[i],
                     rel_bias, w_attn_up[i], shift_mix[i], w0[i], w_decay_up[i], a0[i], w_aaa_up[i],
                     w_gate_up[i], k_k[i], k_a[i], r_k[i], gn_w[i], gn_b[i], w_rwkv_up[i], w_out[i],
                     norm_mlp[i], w_mlp_in[i], w_mlp_out[i], norm_ple[i], w_ple_gate[i], w_ple_proj[i])
    return x2d.reshape(batch, seq, d_model)
```

```python
import functools
import math

import jax
import jax.numpy as jnp
from jax import lax
from jax.experimental import pallas as pl
from jax.experimental.pallas import tpu as pltpu

F32 = jnp.float32
BF16 = jnp.bfloat16

LANES = 128
SUBLANES = 8
MXU_DIM = 256
ATTN_HEAD_DIM = 128
ATTN_HEADS_PER_GROUP = 4
DILATED_GROUPS = ((128, 1), (512, 4), (2048, 16))
ATTN_BLOCK = 128
ATTN_UNITS_PER_STEP = 4
N_BUCKETS = 32
MAX_DISTANCE = 2048
RWKV_HEAD_DIM = 64
DECAY_LORA = 96
AAA_LORA = 96
GATE_LORA = 256
RMS_EPS = 1e-6
GN_EPS = 64e-5
MASK_VALUE = -1e30

RWKV_CHUNK = 64
RWKV_TBLK = 256
RWKV_HPB = 16

VMEM_LIMIT = 56 * 1024 * 1024


def _cparams(sem):
    return pltpu.CompilerParams(dimension_semantics=sem, vmem_limit_bytes=VMEM_LIMIT)


def _dot(a, b):
    return jnp.dot(a, b, preferred_element_type=F32)


def _dot_nt(a, b):
    return lax.dot_general(a, b, (((1,), (1,)), ((), ())), preferred_element_type=F32)


def _dot_split2(x, w_bf16):
    hi = x.astype(BF16)
    lo = (x - hi.astype(F32)).astype(BF16)
    return _dot(hi, w_bf16) + _dot(lo, w_bf16)


def _dot_split3_left(w_bf16, x):
    hi = x.astype(BF16)
    r1 = x - hi.astype(F32)
    mid = r1.astype(BF16)
    lo = (r1 - mid.astype(F32)).astype(BF16)
    return _dot(w_bf16, hi) + _dot(w_bf16, mid) + _dot(w_bf16, lo)


def _rmsnorm_kernel(x_ref, g_ref, o_ref):
    x = x_ref[...]
    ms = jnp.mean(x * x, axis=-1, keepdims=True)
    o_ref[...] = (x * lax.rsqrt(ms + RMS_EPS) * g_ref[...]).astype(o_ref.dtype)


def _rmsnorm_bf16(x, gain, tr=256):
    m, d = x.shape
    return pl.pallas_call(
        _rmsnorm_kernel,
        grid=(m // tr,),
        in_specs=[pl.BlockSpec((tr, d), lambda i: (i, 0)), pl.BlockSpec((1, d), lambda i: (0, 0))],
        out_specs=pl.BlockSpec((tr, d), lambda i: (i, 0)),
        out_shape=jax.ShapeDtypeStruct((m, d), BF16),
        compiler_params=_cparams(("parallel",)),
        name="rmsnorm",
    )(x, gain.reshape(1, d))


def _qkv_kernel(a_ref, w_ref, g_ref, o_ref, scr_ref, *, dilation):
    j = pl.program_id(1)
    acc = _dot_nt(a_ref[...], w_ref[...].astype(BF16))
    rows = acc.shape[0] // dilation
    for h in range(ATTN_HEADS_PER_GROUP):
        y = acc[:, h * LANES:(h + 1) * LANES]
        ms = jnp.mean(y * y, axis=-1, keepdims=True)
        y = y * jnp.where(j < 2, lax.rsqrt(ms + RMS_EPS), 1.0) * g_ref[...]
        if dilation == 1:
            o_ref[h] = y.astype(o_ref.dtype)
        else:
            scr_ref[h] = y
            for r in range(dilation):
                o_ref[h, :, r * LANES:(r + 1) * LANES] = (
                    scr_ref[h, pl.ds(r, rows, stride=dilation), :].astype(o_ref.dtype))


def _qkv_proj(h, w_in_t, gains, group, dilation, batch, seq, tm=1024):
    m, d = h.shape
    tn = ATTN_HEADS_PER_GROUP * LANES
    mt_per_b = seq // tm
    hg = ATTN_HEADS_PER_GROUP
    n_groups = len(DILATED_GROUPS)
    return pl.pallas_call(
        functools.partial(_qkv_kernel, dilation=dilation),
        grid=(m // tm, 3),
        in_specs=[
            pl.BlockSpec((tm, d), lambda i, j: (i, 0)),
            pl.BlockSpec((tn, d), lambda i, j: (j * n_groups + group, 0)),
            pl.BlockSpec((None, 1, LANES), lambda i, j: (j, 0, 0)),
        ],
        out_specs=pl.BlockSpec((None, hg, tm // dilation, dilation * LANES),
                               lambda i, j: (i // mt_per_b, j, i % mt_per_b, 0)),
        out_shape=jax.ShapeDtypeStruct((batch, 3 * hg, seq // dilation, dilation * LANES), BF16),
        scratch_shapes=[pltpu.VMEM((hg, tm, LANES), F32)],
        compiler_params=_cparams(("parallel", "arbitrary")),
        name=f"w_in_qkv_d{dilation}",
    )(h, w_in_t, gains)


def _mm_kernel(a_ref, w_ref, o_ref, *, act, transposed_w):
    w = w_ref[...].astype(BF16)
    acc = _dot_nt(a_ref[...], w) if transposed_w else _dot(a_ref[...], w)
    if act == "sigmoid":
        acc = jax.nn.sigmoid(acc)
    elif act == "relu2":
        acc = jnp.square(jnp.maximum(acc, 0.0))
    o_ref[...] = acc.astype(o_ref.dtype)


def _matmul(a, w, *, n=None, wt_row0=None, act=None, out_dtype=F32, tm=1024, tn=512, name="matmul"):
    m, k = a.shape
    transposed_w = wt_row0 is not None
    n = w.shape[1] if n is None else n
    if transposed_w:
        assert wt_row0 % SUBLANES == 0 and tn % SUBLANES == 0
        w_spec = pl.BlockSpec((pl.Element(tn), pl.Element(k)),
                              lambda i, j: (pl.multiple_of(wt_row0 + j * tn, SUBLANES), 0))
    else:
        w_spec = pl.BlockSpec((k, tn), lambda i, j: (0, j))
    return pl.pallas_call(
        functools.partial(_mm_kernel, act=act, transposed_w=transposed_w),
        grid=(m // tm, n // tn),
        in_specs=[pl.BlockSpec((tm, k), lambda i, j: (i, 0)), w_spec],
        out_specs=pl.BlockSpec((tm, tn), lambda i, j: (i, j)),
        out_shape=jax.ShapeDtypeStruct((m, n), out_dtype),
        compiler_params=_cparams(("parallel", "arbitrary")),
        name=name,
    )(a, w)


def _mm_shift_kernel(a_ref, w_ref, mix_ref, o_ref, carry_ref, *, m_tiles_per_seq):
    i = pl.program_id(0)
    j = pl.program_id(1)

    @pl.when(i == 0)
    def _():
        carry_ref[j] = jnp.zeros(carry_ref.shape[1:], F32)

    z = _dot_nt(a_ref[...], w_ref[...].astype(BF16))
    tm = z.shape[0]
    row = lax.broadcasted_iota(jnp.int32, (tm, 1), 0)
    sublanes = carry_ref.shape[1]
    prev_row = jnp.where(i % m_tiles_per_seq == 0, 0.0, carry_ref[j, sublanes - 1:sublanes, :])
    prev = jnp.where(row == 0, prev_row, pltpu.roll(z, 1, axis=0))
    o_ref[...] = z + mix_ref[...] * (prev - z)
    carry_ref[j] = z[tm - sublanes:tm, :]


def _matmul_token_shift(a, wt, mix, seq, *, n, wt_row_blk, tm=1024, tn=512, name="matmul_shift"):
    m, k = a.shape
    n_tiles = n // tn
    return pl.pallas_call(
        functools.partial(_mm_shift_kernel, m_tiles_per_seq=seq // tm),
        grid=(m // tm, n_tiles),
        in_specs=[pl.BlockSpec((tm, k), lambda i, j: (i, 0)),
                  pl.BlockSpec((tn, k), lambda i, j: (wt_row_blk + j, 0)),
                  pl.BlockSpec((1, tn), lambda i, j: (0, j))],
        out_specs=pl.BlockSpec((tm, tn), lambda i, j: (i, j)),
        out_shape=jax.ShapeDtypeStruct((m, n), F32),
        scratch_shapes=[pltpu.VMEM((n_tiles, SUBLANES, tn), F32)],
        compiler_params=_cparams(("arbitrary", "arbitrary")),
        name=name,
    )(a, wt, mix)


def _mm_res_kernel(a_ref, w_ref, r_ref, o_ref):
    o_ref[...] = r_ref[...] + _dot(a_ref[...], w_ref[...].astype(BF16))


def _matmul_residual(a, w, res, *, tm=1024, tn=512, name="matmul_res"):
    m, k = a.shape
    n = w.shape[1]
    return pl.pallas_call(
        _mm_res_kernel,
        grid=(m // tm, n // tn),
        in_specs=[
            pl.BlockSpec((tm, k), lambda i, j: (i, 0)),
            pl.BlockSpec((k, tn), lambda i, j: (0, j)),
            pl.BlockSpec((tm, tn), lambda i, j: (i, j)),
        ],
        out_specs=pl.BlockSpec((tm, tn), lambda i, j: (i, j)),
        out_shape=jax.ShapeDtypeStruct((m, n), F32),
        compiler_params=_cparams(("parallel", "arbitrary")),
        name=name,
    )(a, w, res)


def _mm_kacc_res_kernel(a_ref, w_ref, r_ref, o_ref):
    k = pl.program_id(2)

    @pl.when(k == 0)
    def _():
        o_ref[...] = r_ref[...] + _dot(a_ref[...], w_ref[...].astype(BF16))

    @pl.when(k > 0)
    def _():
        o_ref[...] += _dot(a_ref[...], w_ref[...].astype(BF16))


def _matmul_kacc_residual(a, w, res, *, tm=1024, tn=1024, tk=2048, name="matmul_kacc"):
    m, kdim = a.shape
    n = w.shape[1]
    return pl.pallas_call(
        _mm_kacc_res_kernel,
        grid=(m // tm, n // tn, kdim // tk),
        in_specs=[
            pl.BlockSpec((tm, tk), lambda i, j, k: (i, k)),
            pl.BlockSpec((tk, tn), lambda i, j, k: (k, j)),
            pl.BlockSpec((tm, tn), lambda i, j, k: (i, j)),
        ],
        out_specs=pl.BlockSpec((tm, tn), lambda i, j, k: (i, j)),
        out_shape=jax.ShapeDtypeStruct((m, n), F32),
        compiler_params=_cparams(("parallel", "parallel", "arbitrary")),
        name=name,
    )(a, w, res)


def _t5_bucket(dist):
    max_exact = N_BUCKETS // 2
    d_f = jnp.maximum(dist, 1).astype(F32)
    large = max_exact + (jnp.log(d_f / max_exact) / math.log(MAX_DISTANCE / max_exact)
                         * (N_BUCKETS - max_exact)).astype(jnp.int32)
    large = jnp.minimum(large, N_BUCKETS - 1)
    return jnp.where(dist < max_exact, dist, large)


def _band_buckets():
    blk = ATTN_BLOCK
    rel = (blk + jnp.arange(blk))[:, None] - jnp.arange(2 * blk)[None, :]
    out = []
    for window, dilation in DILATED_GROUPS:
        band = (rel >= 0) & (rel <= window // dilation)
        out.append(jnp.where(band, _t5_bucket(jnp.maximum(rel, 0) * dilation), -1))
    return jnp.stack(out).astype(jnp.int32)


def _attn_kernel(q0, k0, v0, q1, k1, v1, q2, k2, v2, bucket_ref, tab_ref, o_ref, oacc_ref, lacc_ref):
    blk = ATTN_BLOCK
    hh = pl.program_id(1)
    seq = oacc_ref.shape[0]
    refs = ((q0, k0, v0), (q1, k1, v1), (q2, k2, v2))

    def softmax_out(s, v):
        m = jnp.max(s, axis=-1, keepdims=True)
        e = jnp.exp(s - m)
        l = jnp.sum(e, axis=-1, keepdims=True)
        o = _dot(e.astype(BF16), v) / l
        return o, jnp.broadcast_to(m + jnp.log(l), (blk, LANES))

    for g, (_, dilation) in enumerate(DILATED_GROUPS):
        q_ref, k_ref, v_ref = refs[g]
        n_blocks = seq // (dilation * blk)
        bucket = bucket_ref[g]
        head = g * ATTN_HEADS_PER_GROUP + hh
        bias = jnp.where(bucket < 0, MASK_VALUE, 0.0)
        for b in range(N_BUCKETS):
            bias = jnp.where(bucket == b, tab_ref[b, head], bias)

        def merge_store(tok_rows, o, lse, g=g):
            if g == 0:
                oacc_ref[tok_rows, :] = o
                lacc_ref[tok_rows, :] = lse
            else:
                o_old = oacc_ref[tok_rows, :]
                l_old = lacc_ref[tok_rows, :]
                m = jnp.maximum(l_old, lse)
                w_old = jnp.exp(l_old - m)
                w_new = jnp.exp(lse - m)
                den = w_old + w_new
                oacc_ref[tok_rows, :] = (w_old * o_old + w_new * o) / den
                lacc_ref[tok_rows, :] = m + jnp.log(den)

        def tok_rows(off, r, dilation=dilation):
            if dilation == 1:
                return pl.ds(off, blk)
            return pl.ds(off * dilation + r, blk, stride=dilation)

        def unit(off, r, q_ref=q_ref, k_ref=k_ref, v_ref=v_ref, bias=bias, merge_store=merge_store,
                 tok_rows=tok_rows):
            cs = slice(r * LANES, (r + 1) * LANES)
            s = _dot_nt(q_ref[pl.ds(off, blk), cs], k_ref[pl.ds(off - blk, 2 * blk), cs]) + bias
            o, lse = softmax_out(s, v_ref[pl.ds(off - blk, 2 * blk), cs])
            merge_store(tok_rows(off, r), o, lse)

        for r in range(dilation):
            cs = slice(r * LANES, (r + 1) * LANES)
            s0 = _dot_nt(q_ref[0:blk, cs], k_ref[0:blk, cs]) + bias[:, blk:]
            o0, lse0 = softmax_out(s0, v_ref[0:blk, cs])
            merge_store(tok_rows(0, r), o0, lse0)

        ways = ATTN_UNITS_PER_STEP
        if dilation == 1:
            part = n_blocks // ways
            for w in range(1, ways):
                unit(w * part * blk, 0)

            def body(n, carry, unit=unit, part=part):
                off = pl.multiple_of(n * blk, blk)
                for w in range(ways):
                    unit(off + w * part * blk, 0)
                return carry

            lax.fori_loop(1, part, body, 0)
        else:
            for r in range(0, dilation, ways):
                def body(n, carry, unit=unit, r=r):
                    off = pl.multiple_of(n * blk, blk)
                    for w in range(ways):
                        unit(off, r + w)
                    return carry

                lax.fori_loop(1, n_blocks, body, 0)

    o_ref[...] = oacc_ref[...].astype(o_ref.dtype)


def _dilated_attention(qkv_groups, rel_bias, batch, seq):
    hg = ATTN_HEADS_PER_GROUP
    in_specs, operands = [], []
    for (_, dilation), arr in zip(DILATED_GROUPS, qkv_groups):
        for section in range(3):
            in_specs.append(pl.BlockSpec((None, None, seq // dilation, dilation * LANES),
                                         lambda b, h, section=section: (b, section * hg + h, 0, 0)))
            operands.append(arr)
    buckets = _band_buckets()
    in_specs.append(pl.BlockSpec(buckets.shape, lambda b, h: (0, 0, 0)))
    in_specs.append(pl.BlockSpec(memory_space=pltpu.SMEM))
    return pl.pallas_call(
        _attn_kernel,
        grid=(batch, hg),
        in_specs=in_specs,
        out_specs=pl.BlockSpec((seq, LANES), lambda b, h: (b, h)),
        out_shape=jax.ShapeDtypeStruct((batch * seq, hg * LANES), BF16),
        scratch_shapes=[pltpu.VMEM((seq, LANES), F32), pltpu.VMEM((seq, LANES), F32)],
        compiler_params=_cparams(("parallel", "parallel")),
        name="dilated_attn",
    )(*operands, buckets, rel_bias.astype(F32))


_STAGE_BF16 = ("at", "rt", "bt", "kt", "bh", "kh", "vb")
_STAGE_F32 = ("egc", "bonus", "gate")
_STAGE_NAMES = _STAGE_BF16 + _STAGE_F32
_N_HANDOFF = 9


def _seg_sum(x, seg):
    w = seg.shape[0]
    return jnp.concatenate([_dot_split2(x[:, i * w:(i + 1) * w], seg) for i in range(x.shape[1] // w)], axis=1)


def _rwkv_prepare(r_ref, k_ref, v_ref, xl_ref, wl_ref, w0_ref, a0_ref, kk_ref, ka_ref, rk_ref, seg_ref, tri_ref,
                  stage, lora_dims):
    tblk, cw = r_ref.shape
    ch = RWKV_CHUNK
    r, k, v, xl = r_ref[...], k_ref[...], v_ref[...], xl_ref[...]

    d_w, d_a, _ = lora_dims
    lane_l = lax.broadcasted_iota(jnp.int32, xl.shape, 1)
    act = jnp.where(lane_l < d_w, jnp.tanh(xl), jnp.where(lane_l < d_w + d_a, xl, jax.nn.sigmoid(xl)))
    up = _dot(act.astype(BF16), wl_ref[...])
    w_arg = -(w0_ref[...] + up[:, 0:cw])
    softplus = jnp.maximum(w_arg, 0.0) + jnp.log1p(jnp.exp(-jnp.abs(w_arg)))
    logdecay = -jnp.exp(-softplus - 0.5)
    a_gate = jax.nn.sigmoid(a0_ref[...] + up[:, cw:2 * cw])
    stage["gate"][...] = up[:, 2 * cw:3 * cw]
    yield

    seg = seg_ref[...]
    kk = k * kk_ref[...]
    kk = kk * lax.rsqrt(jnp.maximum(_seg_sum(kk * kk, seg), 1e-24))
    k2 = k * (1.0 + (a_gate - 1.0) * ka_ref[...])
    a_vec = -kk
    b_vec = kk * a_gate
    stage["bonus"][...] = _seg_sum(r * k2 * rk_ref[...], seg) * v
    yield

    tri = tri_ref[...]
    for c in range(tblk // ch):
        rows = slice(c * ch, (c + 1) * ch)
        ld = logdecay[rows]
        g_inc = _dot_split3_left(tri, ld)
        g_exc = g_inc - ld
        e_neg = jnp.exp(-g_inc)
        e_last = jnp.exp(g_inc[ch - 1:ch, :])
        bt = b_vec[rows] * e_neg
        kt = k2[rows] * e_neg
        stage["at"][rows, :] = (a_vec[rows] * jnp.exp(g_exc)).astype(BF16)
        stage["rt"][rows, :] = (r[rows] * jnp.exp(g_inc)).astype(BF16)
        stage["bt"][rows, :] = bt.astype(BF16)
        stage["kt"][rows, :] = kt.astype(BF16)
        stage["bh"][rows, :] = (bt * e_last).astype(BF16)
        stage["kh"][rows, :] = (kt * e_last).astype(BF16)
        stage["egc"][c] = jnp.broadcast_to(e_last, stage["egc"].shape[1:])
        yield
    stage["vb"][...] = v.astype(BF16)


def _rwkv_solve(stage, state_ref, y_ref, o_ref, gw_ref, gb_ref, seg_ref, handoff, zero_off):
    tblk, cw = o_ref.shape
    ch = RWKV_CHUNK
    n_chunks = tblk // ch
    n_pairs = cw // LANES
    hd = RWKV_HEAD_DIM

    lane = lax.broadcasted_iota(jnp.int32, (1, LANES), 1)
    m0, m1 = lane < hd, lane >= hd
    lane2 = lax.broadcasted_iota(jnp.int32, (1, 2 * LANES), 1) % LANES
    m0w, m1w = lane2 < hd, lane2 >= hd
    rr = lax.broadcasted_iota(jnp.int32, (ch, 2 * LANES), 0)
    cc = lax.broadcasted_iota(jnp.int32, (ch, 2 * LANES), 1) % ch
    strict_lower = cc < rr
    incl_lower = cc <= rr
    eye2 = (lax.broadcasted_iota(jnp.int32, (ch, LANES), 0)
            == lax.broadcasted_iota(jnp.int32, (ch, LANES), 1) % ch).astype(F32)
    bd_r = lax.broadcasted_iota(jnp.int32, (LANES, LANES), 0) >= hd
    bd_c = lax.broadcasted_iota(jnp.int32, (LANES, LANES), 1) >= hd
    block_diag = bd_r == bd_c
    zeros_cv = jnp.zeros((ch, LANES), BF16)

    def keep(mask, x):
        return jnp.where(mask, x, jnp.zeros_like(x))

    problems = [(c, p) for c in range(n_chunks) for p in range(n_pairs)]

    def blk(name, cp):
        c, p = cp
        return stage[name][c * ch:(c + 1) * ch, p * LANES:(p + 1) * LANES]

    aakv_s, tcat_s, g2_s, wu_s, ry_s, rt2_s, y0_s, mlr_s, n0_s = handoff

    g1, pw, tinv = {}, {}, {}
    for i, cp in enumerate(problems):
        bt, kt = blk("bt", cp), blk("kt", cp)
        x = jnp.concatenate([blk("at", cp), blk("rt", cp)], axis=0)
        ym = jnp.concatenate([keep(m0, bt), keep(m0, kt), keep(m1, kt), keep(m1, bt)], axis=0)
        g_all = _dot_nt(x, ym)
        g1[cp] = jnp.where(strict_lower, g_all[:ch], 0.0)
        g2_s[i] = jnp.where(incl_lower, g_all[ch:], 0.0).astype(BF16)
        pw[cp] = jnp.where(m0, g1[cp][:, :LANES], g1[cp][:, LANES:])
        tinv[cp] = eye2 + pw[cp]
    yield

    def block_diag_of(cat):
        cat = cat.astype(BF16)
        return jnp.concatenate([keep(m0, cat), keep(m1, cat)], axis=0)

    n_sq = int(math.log2(ch)) - 1
    for cp in problems:
        pw[cp] = _dot(pw[cp].astype(BF16), block_diag_of(pw[cp]))
    yield
    for _ in range(n_sq - 1):
        for cp in problems:
            both = _dot(jnp.concatenate([tinv[cp], pw[cp]], axis=0).astype(BF16), block_diag_of(pw[cp]))
            tinv[cp] = tinv[cp] + both[:ch]
            pw[cp] = both[ch:]
        yield
    for cp in problems:
        tinv[cp] = tinv[cp] + _dot(tinv[cp].astype(BF16), block_diag_of(pw[cp]))
    yield
    for i, cp in enumerate(problems):
        vb = blk("vb", cp)
        v_stack = jnp.concatenate([zeros_cv, keep(m0, vb), keep(m1, vb), zeros_cv], axis=0)
        aakv_s[i] = _dot(g1[cp].astype(BF16), v_stack).astype(BF16)
        tcat_s[i] = tinv[cp].astype(BF16)
    yield
    for i, cp in enumerate(problems):
        j = zero_off + i
        rhs = jnp.concatenate([blk("at", cp), aakv_s[j]], axis=1)
        wu_s[i] = _dot(tcat_s[j], jnp.concatenate([keep(m0w, rhs), keep(m1w, rhs)], axis=0))
    yield
    for i, cp in enumerate(problems):
        j = zero_off + i
        upper = wu_s[j].astype(BF16)
        lower = jnp.concatenate([zeros_cv, blk("vb", cp)], axis=1)
        rhs = jnp.concatenate([keep(m0w, upper), keep(m0w, lower), keep(m1w, lower), keep(m1w, upper)], axis=0)
        ry_s[i] = _dot(g2_s[j], rhs)
    yield
    for i, cp in enumerate(problems):
        j = zero_off + i
        wu, ry = wu_s[j], ry_s[j]
        bkh = jnp.concatenate([blk("bh", cp), blk("kh", cp)], axis=0)
        rt2_s[i] = (blk("rt", cp).astype(F32) + ry[:, :LANES]).astype(BF16)
        y0_s[i] = ry[:, LANES:]
        wt_t = wu[:, :LANES].T.astype(BF16)
        mlr_s[i] = jnp.where(block_diag, _dot(wt_t, bkh[:ch]), 0.0).astype(BF16)
        uv_t = jnp.concatenate([wu[:, LANES:], blk("vb", cp).astype(F32)], axis=0).T.astype(BF16)
        n0_s[i] = jnp.where(block_diag, _dot(uv_t, bkh), 0.0)
    yield

    for c in range(n_chunks):
        rows = slice(c * ch, (c + 1) * ch)
        for p in range(n_pairs):
            cs = slice(p * LANES, (p + 1) * LANES)
            j = zero_off + (c * n_pairs + p)
            ht = state_ref[p]
            htb = ht.astype(BF16)
            y_ref[rows, cs] = _dot_nt(rt2_s[j], htb) + y0_s[j]
            state_ref[p] = ht * stage["egc"][c, 0:1, cs] + _dot(htb, mlr_s[j]) + n0_s[j]
    yield

    seg = seg_ref[...]
    y = y_ref[...]
    inv_hd = 1.0 / hd
    mu = _seg_sum(y, seg) * inv_hd
    yc = y - mu
    var = _seg_sum(yc * yc, seg) * inv_hd
    yn = yc * lax.rsqrt(var + GN_EPS) * gw_ref[...] + gb_ref[...]
    o_ref[...] = ((yn + stage["bonus"][...]) * stage["gate"][...]).astype(o_ref.dtype)


def _interleave(streams, order):
    for i in order:
        next(streams[i], None)
    for g in streams:
        for _ in g:
            pass


_RWKV_TRACE_ORDER = (0, 1, 0, 0, 1, 0, 0, 0, 0, 0, 1, 0, 1, 0, 1, 0, 1, 0, 0)


def _rwkv_kernel(*refs, lora_dims):
    n_in = 15
    (r_ref, k_ref, v_ref, xl_ref, wl_ref, w0_ref, a0_ref, kk_ref, ka_ref, rk_ref, gw_ref, gb_ref, seg_ref,
     tri_ref, zero_ref) = refs[:n_in]
    o_ref = refs[n_in]
    scratch = refs[n_in + 1:]
    state_ref, y_ref = scratch[:2]
    handoff = scratch[2:2 + _N_HANDOFF]
    ns = len(_STAGE_NAMES)
    stages = [dict(zip(_STAGE_NAMES, scratch[2 + _N_HANDOFF + i * ns: 2 + _N_HANDOFF + (i + 1) * ns]))
              for i in range(2)]
    t = pl.program_id(2)

    @pl.when(t == 0)
    def _():
        state_ref[...] = jnp.zeros_like(state_ref)
        for ref in stages[1].values():
            ref[...] = jnp.zeros_like(ref)

    for parity in range(2):
        @pl.when(t % 2 == parity)
        def _(parity=parity):
            _interleave([
                _rwkv_solve(stages[1 - parity], state_ref, y_ref, o_ref, gw_ref, gb_ref, seg_ref, handoff,
                            zero_ref[0]),
                _rwkv_prepare(r_ref, k_ref, v_ref, xl_ref, wl_ref, w0_ref, a0_ref, kk_ref, ka_ref, rk_ref,
                              seg_ref, tri_ref, stages[parity], lora_dims)], _RWKV_TRACE_ORDER)


def _rwkv7(zr, zl, w0, w_decay_up, a0, w_aaa_up, w_gate_up, k_k, k_a, r_k, gn_w, gn_b, batch, seq):
    tok = zr.shape[0]
    width = zr.shape[1] // 3
    lw = zl.shape[1]
    cw = RWKV_HPB * RWKV_HEAD_DIM
    nhb = width // cw
    tblk = RWKV_TBLK
    nt = seq // tblk
    ch = RWKV_CHUNK

    def pad_rows(w, start):
        return jnp.pad(w, ((start, lw - start - w.shape[0]), (0, 0)))

    wl = jnp.stack([pad_rows(w_decay_up, 0), pad_rows(w_aaa_up, DECAY_LORA),
                    pad_rows(w_gate_up, DECAY_LORA + AAA_LORA)], axis=0)
    wl = wl.reshape(3, lw, nhb, cw).transpose(2, 1, 0, 3).reshape(nhb, lw, 3 * cw).astype(BF16)

    idx = jnp.arange(MXU_DIM) // RWKV_HEAD_DIM
    seg = (idx[:, None] == idx[None, :]).astype(BF16)
    tri = (jnp.arange(ch)[:, None] >= jnp.arange(ch)[None, :]).astype(BF16)

    def in_blk(b, t):
        return b * nt + jnp.minimum(t, nt - 1)

    def col(sec):
        return pl.BlockSpec((tblk, cw), lambda b, h, t: (in_blk(b, t), sec * nhb + h))

    vec = pl.BlockSpec((1, cw), lambda b, h, t: (0, h))
    row_vec = lambda a: a.reshape(1, width)
    stage_shapes = ([pltpu.VMEM((tblk, cw), BF16)] * len(_STAGE_BF16)
                    + [pltpu.VMEM((tblk // ch, 8, cw), F32),
                       pltpu.VMEM((tblk, cw), F32), pltpu.VMEM((tblk, cw), F32)])
    n_prob = (tblk // ch) * (cw // LANES)
    handoff_shapes = [
        pltpu.VMEM((n_prob, ch, LANES), BF16),
        pltpu.VMEM((n_prob, ch, LANES), BF16),
        pltpu.VMEM((n_prob, ch, 2 * LANES), BF16),
        pltpu.VMEM((n_prob, ch, 2 * LANES), F32),
        pltpu.VMEM((n_prob, ch, 2 * LANES), F32),
        pltpu.VMEM((n_prob, ch, LANES), BF16), pltpu.VMEM((n_prob, ch, LANES), F32),
        pltpu.VMEM((n_prob, LANES, LANES), BF16), pltpu.VMEM((n_prob, LANES, LANES), F32),
    ]
    assert len(handoff_shapes) == _N_HANDOFF
    kern = functools.partial(_rwkv_kernel, lora_dims=(DECAY_LORA, AAA_LORA, GATE_LORA))
    return pl.pallas_call(
        kern,
        grid=(batch, nhb, nt + 1),
        in_specs=[
            col(0), col(1), col(2),
            pl.BlockSpec((tblk, lw), lambda b, h, t: (in_blk(b, t), 0)),
            pl.BlockSpec((None, lw, 3 * cw), lambda b, h, t: (h, 0, 0)),
            vec, vec, vec, vec, vec, vec, vec,
            pl.BlockSpec((MXU_DIM, MXU_DIM), lambda b, h, t: (0, 0)),
            pl.BlockSpec((ch, ch), lambda b, h, t: (0, 0)),
            pl.BlockSpec(memory_space=pltpu.SMEM),
        ],
        out_specs=pl.BlockSpec((tblk, cw), lambda b, h, t: (b * nt + jnp.maximum(t - 1, 0), h)),
        out_shape=jax.ShapeDtypeStruct((tok, width), BF16),
        scratch_shapes=[
            pltpu.VMEM((cw // LANES, LANES, LANES), F32),
            pltpu.VMEM((tblk, cw), F32),
        ] + handoff_shapes + stage_shapes * 2,
        compiler_params=_cparams(("parallel", "parallel", "arbitrary")),
        name="rwkv7_chunked",
    )(zr, zr, zr, zl, wl,
      row_vec(w0), row_vec(a0), row_vec(k_k), row_vec(k_a), row_vec(r_k), row_vec(gn_w), row_vec(gn_b),
      seg, tri, jnp.zeros((1,), jnp.int32))


def _merge_kernel(attn_ref, rw_ref, g0_ref, g1_ref, wa_ref, wr_ref, out_ref):
    attn_d = _dot(attn_ref[...], wa_ref[...].astype(BF16))
    rwkv_d = _dot(rw_ref[...], wr_ref[...].astype(BF16))
    out_ref[...] = (g0_ref[...].astype(F32) * attn_d + g1_ref[...].astype(F32) * rwkv_d).astype(out_ref.dtype)


def _merge(attn, rw, gates, w_attn_up, w_rwkv_up, tm=1024, tn=512):
    tok = rw.shape[0]
    d = w_attn_up.shape[1]
    nj = d // tn
    return pl.pallas_call(
        _merge_kernel,
        grid=(tok // tm, nj),
        in_specs=[
            pl.BlockSpec((tm, attn.shape[1]), lambda i, j: (i, 0)),
            pl.BlockSpec((tm, rw.shape[1]), lambda i, j: (i, 0)),
            pl.BlockSpec((tm, tn), lambda i, j: (i, j)),
            pl.BlockSpec((tm, tn), lambda i, j: (i, nj + j)),
            pl.BlockSpec((w_attn_up.shape[0], tn), lambda i, j: (0, j)),
            pl.BlockSpec((w_rwkv_up.shape[0], tn), lambda i, j: (0, j)),
        ],
        out_specs=pl.BlockSpec((tm, tn), lambda i, j: (i, j)),
        out_shape=jax.ShapeDtypeStruct((tok, d), BF16),
        compiler_params=_cparams(("parallel", "arbitrary")),
        name="merge_gated_up",
    )(attn, rw, gates, gates, w_attn_up, w_rwkv_up)


def _ple_kernel(h_ref, wg_ref, p_ref, wp_ref, x_ref, o_ref):
    gate = jax.nn.sigmoid(_dot(h_ref[...], wg_ref[...].astype(BF16)))
    o_ref[...] = x_ref[...] + gate * _dot(p_ref[...], wp_ref[...].astype(BF16))


def _ple(hn, w_gate, p, w_proj, x, tm=1024, tn=512):
    m, k = hn.shape
    n = w_gate.shape[1]
    kp = p.shape[1]
    return pl.pallas_call(
        _ple_kernel,
        grid=(m // tm, n // tn),
        in_specs=[
            pl.BlockSpec((tm, k), lambda i, j: (i, 0)),
            pl.BlockSpec((k, tn), lambda i, j: (0, j)),
            pl.BlockSpec((tm, kp), lambda i, j: (i, 0)),
            pl.BlockSpec((kp, tn), lambda i, j: (0, j)),
            pl.BlockSpec((tm, tn), lambda i, j: (i, j)),
        ],
        out_specs=pl.BlockSpec((tm, tn), lambda i, j: (i, j)),
        out_shape=jax.ShapeDtypeStruct((m, n), F32),
        compiler_params=_cparams(("parallel", "arbitrary")),
        name="ple_gated",
    )(hn, w_gate, p, w_proj, x)


def _layer(x2d, p2d, batch, seq, norm_mix, w_in, q_gain, k_gain, rel_bias, w_attn_up, shift_mix, w0,
           w_decay_up, a0, w_aaa_up, w_gate_up, k_k, k_a, r_k, gn_w, gn_b, w_rwkv_up, w_out, norm_mlp,
           w_mlp_in, w_mlp_out, norm_ple, w_ple_gate, w_ple_proj):
    n_attn_heads = len(DILATED_GROUPS) * ATTN_HEADS_PER_GROUP
    attn_width = n_attn_heads * ATTN_HEAD_DIM
    rwkv_width = w_rwkv_up.shape[0]
    a_end = 3 * attn_width
    z_end = a_end + 3 * rwkv_width
    r_end = z_end + DECAY_LORA + AAA_LORA + GATE_LORA

    h = _rmsnorm_bf16(x2d, norm_mix)

    gains = jnp.stack([q_gain * (ATTN_HEAD_DIM ** -0.5), k_gain, jnp.ones_like(q_gain)]).reshape(3, 1, LANES)
    w_in_t = jnp.swapaxes(w_in, 0, 1)
    tn = ATTN_HEADS_PER_GROUP * ATTN_HEAD_DIM
    qkv_groups = [_qkv_proj(h, w_in_t, gains, gi, dilation, batch, seq)
                  for gi, (_, dilation) in enumerate(DILATED_GROUPS)]
    mix_z = shift_mix[: z_end - a_end].reshape(1, -1)
    zr = _matmul_token_shift(h, w_in_t, mix_z, seq, n=z_end - a_end, wt_row_blk=a_end // tn, tn=tn,
                             name="w_in_rkv")
    mix_l = jnp.pad(shift_mix[z_end - a_end:], (0, tn - (r_end - z_end))).reshape(1, tn)
    zl = _matmul_token_shift(h, w_in_t, mix_l, seq, n=tn, wt_row_blk=z_end // tn, tn=tn, name="w_in_lora")
    gates = _matmul(h, w_in_t, n=w_in.shape[1] - r_end, wt_row0=r_end, act="sigmoid", out_dtype=BF16,
                    name="w_in_gates")

    attn = _dilated_attention(qkv_groups, rel_bias, batch, seq)

    rw = _rwkv7(zr, zl, w0, w_decay_up, a0, w_aaa_up, w_gate_up, k_k, k_a, r_k, gn_w, gn_b, batch, seq)

    merged = _merge(attn, rw, gates, w_attn_up, w_rwkv_up)
    x1 = _matmul_residual(merged, w_out, x2d, name="w_out_res")

    h2 = _rmsnorm_bf16(x1, norm_mlp)
    u = _matmul(h2, w_mlp_in, act="relu2", out_dtype=BF16, name="mlp_in")
    x2 = _matmul_kacc_residual(u, w_mlp_out, x1, name="mlp_out_res")

    hn = _rmsnorm_bf16(x2, norm_ple)
    return _ple(hn, w_ple_gate, p2d.astype(BF16), w_ple_proj, x2)


def kernel(x, p, norm_mix, w_in, q_gain, k_gain, rel_bias, w_attn_up, shift_mix, w0, w_decay_up, a0, w_aaa_up,
           w_gate_up, k_k, k_a, r_k, gn_w, gn_b, w_rwkv_up, w_out, norm_mlp, w_mlp_in, w_mlp_out, norm_ple,
           w_ple_gate, w_ple_proj):
    batch, seq, d_model = x.shape
    depth = p.shape[0]
    x2d = x.reshape(batch * seq, d_model)
    for i in range(depth):
        x2d = _layer(x2d, p[i].reshape(batch * seq, -1), batch, seq, norm_mix[i], w_in[i], q_gain[i], k_gain[i],
                     rel_bias, w_attn_up[i], shift_mix[i], w0[i], w_decay_up[i], a0[i], w_aaa_up[i],
                     w_gate_up[i], k_k[i], k_a[i], r_k[i], gn_w[i], gn_b[i], w_rwkv_up[i], w_out[i],
                     norm_mlp[i], w_mlp_in[i], w_mlp_out[i], norm_ple[i], w_ple_gate[i], w_ple_proj[i])
    return x2d.reshape(batch, seq, d_model)
```

```python
import functools
import math

import jax
import jax.numpy as jnp
from jax import lax
from jax.experimental import pallas as pl
from jax.experimental.pallas import tpu as pltpu

F32 = jnp.float32
BF16 = jnp.bfloat16

LANES = 128
SUBLANES = 8
MXU_DIM = 256
ATTN_HEAD_DIM = 128
ATTN_HEADS_PER_GROUP = 4
DILATED_GROUPS = ((128, 1), (512, 4), (2048, 16))
ATTN_BLOCK = 128
ATTN_UNITS_PER_STEP = 4
N_BUCKETS = 32
MAX_DISTANCE = 2048
RWKV_HEAD_DIM = 64
DECAY_LORA = 96
AAA_LORA = 96
GATE_LORA = 256
RMS_EPS = 1e-6
GN_EPS = 64e-5
MASK_VALUE = -1e30

RWKV_CHUNK = 64
RWKV_TBLK = 256
RWKV_HPB = 16

VMEM_LIMIT = 56 * 1024 * 1024


def _cparams(sem):
    return pltpu.CompilerParams(dimension_semantics=sem, vmem_limit_bytes=VMEM_LIMIT)


def _dot(a, b):
    return jnp.dot(a, b, preferred_element_type=F32)


def _dot_nt(a, b):
    return lax.dot_general(a, b, (((1,), (1,)), ((), ())), preferred_element_type=F32)


def _sigmoid(x):
    return 0.5 * jnp.tanh(0.5 * x) + 0.5


def _dot_split2(x, w_bf16):
    hi = x.astype(BF16)
    lo = (x - hi.astype(F32)).astype(BF16)
    return _dot(hi, w_bf16) + _dot(lo, w_bf16)


def _dot_split3_left(w_bf16, x):
    hi = x.astype(BF16)
    r1 = x - hi.astype(F32)
    mid = r1.astype(BF16)
    lo = (r1 - mid.astype(F32)).astype(BF16)
    return _dot(w_bf16, hi) + _dot(w_bf16, mid) + _dot(w_bf16, lo)


def _rmsnorm_kernel(x_ref, g_ref, o_ref):
    x = x_ref[...]
    ms = jnp.mean(x * x, axis=-1, keepdims=True)
    o_ref[...] = (x * lax.rsqrt(ms + RMS_EPS) * g_ref[...]).astype(o_ref.dtype)


def _rmsnorm_bf16(x, gain, tr=256):
    m, d = x.shape
    return pl.pallas_call(
        _rmsnorm_kernel,
        grid=(m // tr,),
        in_specs=[pl.BlockSpec((tr, d), lambda i: (i, 0)), pl.BlockSpec((1, d), lambda i: (0, 0))],
        out_specs=pl.BlockSpec((tr, d), lambda i: (i, 0)),
        out_shape=jax.ShapeDtypeStruct((m, d), BF16),
        compiler_params=_cparams(("parallel",)),
        name="rmsnorm",
    )(x, gain.reshape(1, d))


def _qkv_kernel(a_ref, w_ref, g_ref, o_ref, scr_ref, *, dilation):
    j = pl.program_id(1)
    acc = _dot_nt(a_ref[...], w_ref[...].astype(BF16))
    rows = acc.shape[0] // dilation
    for h in range(ATTN_HEADS_PER_GROUP):
        y = acc[:, h * LANES:(h + 1) * LANES]
        ms = jnp.mean(y * y, axis=-1, keepdims=True)
        y = y * jnp.where(j < 2, lax.rsqrt(ms + RMS_EPS), 1.0) * g_ref[...]
        if dilation == 1:
            o_ref[h] = y.astype(o_ref.dtype)
        else:
            scr_ref[h] = y
            for r in range(dilation):
                o_ref[h, :, r * LANES:(r + 1) * LANES] = (
                    scr_ref[h, pl.ds(r, rows, stride=dilation), :].astype(o_ref.dtype))


def _qkv_proj(h, w_in_t, gains, group, dilation, batch, seq, tm=1024):
    m, d = h.shape
    tn = ATTN_HEADS_PER_GROUP * LANES
    mt_per_b = seq // tm
    hg = ATTN_HEADS_PER_GROUP
    n_groups = len(DILATED_GROUPS)
    return pl.pallas_call(
        functools.partial(_qkv_kernel, dilation=dilation),
        grid=(m // tm, 3),
        in_specs=[
            pl.BlockSpec((tm, d), lambda i, j: (i, 0)),
            pl.BlockSpec((tn, d), lambda i, j: (j * n_groups + group, 0)),
            pl.BlockSpec((None, 1, LANES), lambda i, j: (j, 0, 0)),
        ],
        out_specs=pl.BlockSpec((None, hg, tm // dilation, dilation * LANES),
                               lambda i, j: (i // mt_per_b, j, i % mt_per_b, 0)),
        out_shape=jax.ShapeDtypeStruct((batch, 3 * hg, seq // dilation, dilation * LANES), BF16),
        scratch_shapes=[pltpu.VMEM((hg, tm, LANES), F32)],
        compiler_params=_cparams(("parallel", "arbitrary")),
        name=f"w_in_qkv_d{dilation}",
    )(h, w_in_t, gains)


def _mm_kernel(a_ref, w_ref, o_ref, *, act, transposed_w):
    w = w_ref[...].astype(BF16)
    acc = _dot_nt(a_ref[...], w) if transposed_w else _dot(a_ref[...], w)
    if act == "sigmoid":
        acc = _sigmoid(acc)
    elif act == "relu2":
        acc = jnp.square(jnp.maximum(acc, 0.0))
    o_ref[...] = acc.astype(o_ref.dtype)


def _matmul(a, w, *, n=None, wt_row0=None, act=None, out_dtype=F32, tm=1024, tn=512, name="matmul"):
    m, k = a.shape
    transposed_w = wt_row0 is not None
    n = w.shape[1] if n is None else n
    if transposed_w:
        assert wt_row0 % SUBLANES == 0 and tn % SUBLANES == 0
        w_spec = pl.BlockSpec((pl.Element(tn), pl.Element(k)),
                              lambda i, j: (pl.multiple_of(wt_row0 + j * tn, SUBLANES), 0))
    else:
        w_spec = pl.BlockSpec((k, tn), lambda i, j: (0, j))
    return pl.pallas_call(
        functools.partial(_mm_kernel, act=act, transposed_w=transposed_w),
        grid=(m // tm, n // tn),
        in_specs=[pl.BlockSpec((tm, k), lambda i, j: (i, 0)), w_spec],
        out_specs=pl.BlockSpec((tm, tn), lambda i, j: (i, j)),
        out_shape=jax.ShapeDtypeStruct((m, n), out_dtype),
        compiler_params=_cparams(("parallel", "arbitrary")),
        name=name,
    )(a, w)


def _mm_shift_kernel(a_ref, w_ref, mix_ref, o_ref, carry_ref, *, m_tiles_per_seq):
    i = pl.program_id(0)
    j = pl.program_id(1)

    @pl.when(i == 0)
    def _():
        carry_ref[j] = jnp.zeros(carry_ref.shape[1:], F32)

    z = _dot_nt(a_ref[...], w_ref[...].astype(BF16))
    tm = z.shape[0]
    row = lax.broadcasted_iota(jnp.int32, (tm, 1), 0)
    sublanes = carry_ref.shape[1]
    prev_row = jnp.where(i % m_tiles_per_seq == 0, 0.0, carry_ref[j, sublanes - 1:sublanes, :])
    prev = jnp.where(row == 0, prev_row, pltpu.roll(z, 1, axis=0))
    o_ref[...] = z + mix_ref[...] * (prev - z)
    carry_ref[j] = z[tm - sublanes:tm, :]


def _matmul_token_shift(a, wt, mix, seq, *, n, wt_row_blk, tm=1024, tn=512, name="matmul_shift"):
    m, k = a.shape
    n_tiles = n // tn
    return pl.pallas_call(
        functools.partial(_mm_shift_kernel, m_tiles_per_seq=seq // tm),
        grid=(m // tm, n_tiles),
        in_specs=[pl.BlockSpec((tm, k), lambda i, j: (i, 0)),
                  pl.BlockSpec((tn, k), lambda i, j: (wt_row_blk + j, 0)),
                  pl.BlockSpec((1, tn), lambda i, j: (0, j))],
        out_specs=pl.BlockSpec((tm, tn), lambda i, j: (i, j)),
        out_shape=jax.ShapeDtypeStruct((m, n), F32),
        scratch_shapes=[pltpu.VMEM((n_tiles, SUBLANES, tn), F32)],
        compiler_params=_cparams(("arbitrary", "arbitrary")),
        name=name,
    )(a, wt, mix)


def _mm_res_kernel(a_ref, w_ref, r_ref, o_ref):
    o_ref[...] = r_ref[...] + _dot(a_ref[...], w_ref[...].astype(BF16))


def _matmul_residual(a, w, res, *, tm=1024, tn=512, name="matmul_res"):
    m, k = a.shape
    n = w.shape[1]
    return pl.pallas_call(
        _mm_res_kernel,
        grid=(m // tm, n // tn),
        in_specs=[
            pl.BlockSpec((tm, k), lambda i, j: (i, 0)),
            pl.BlockSpec((k, tn), lambda i, j: (0, j)),
            pl.BlockSpec((tm, tn), lambda i, j: (i, j)),
        ],
        out_specs=pl.BlockSpec((tm, tn), lambda i, j: (i, j)),
        out_shape=jax.ShapeDtypeStruct((m, n), F32),
        compiler_params=_cparams(("parallel", "arbitrary")),
        name=name,
    )(a, w, res)


def _mm_kacc_res_kernel(a_ref, w_ref, r_ref, o_ref):
    k = pl.program_id(2)

    @pl.when(k == 0)
    def _():
        o_ref[...] = r_ref[...] + _dot(a_ref[...], w_ref[...].astype(BF16))

    @pl.when(k > 0)
    def _():
        o_ref[...] += _dot(a_ref[...], w_ref[...].astype(BF16))


def _matmul_kacc_residual(a, w, res, *, tm=1024, tn=1024, tk=2048, name="matmul_kacc"):
    m, kdim = a.shape
    n = w.shape[1]
    return pl.pallas_call(
        _mm_kacc_res_kernel,
        grid=(m // tm, n // tn, kdim // tk),
        in_specs=[
            pl.BlockSpec((tm, tk), lambda i, j, k: (i, k)),
            pl.BlockSpec((tk, tn), lambda i, j, k: (k, j)),
            pl.BlockSpec((tm, tn), lambda i, j, k: (i, j)),
        ],
        out_specs=pl.BlockSpec((tm, tn), lambda i, j, k: (i, j)),
        out_shape=jax.ShapeDtypeStruct((m, n), F32),
        compiler_params=_cparams(("parallel", "parallel", "arbitrary")),
        name=name,
    )(a, w, res)


def _t5_bucket(dist):
    max_exact = N_BUCKETS // 2
    d_f = jnp.maximum(dist, 1).astype(F32)
    large = max_exact + (jnp.log(d_f / max_exact) / math.log(MAX_DISTANCE / max_exact)
                         * (N_BUCKETS - max_exact)).astype(jnp.int32)
    large = jnp.minimum(large, N_BUCKETS - 1)
    return jnp.where(dist < max_exact, dist, large)


def _band_buckets():
    blk = ATTN_BLOCK
    rel = (blk + jnp.arange(blk))[:, None] - jnp.arange(2 * blk)[None, :]
    out = []
    for window, dilation in DILATED_GROUPS:
        band = (rel >= 0) & (rel <= window // dilation)
        out.append(jnp.where(band, _t5_bucket(jnp.maximum(rel, 0) * dilation), -1))
    return jnp.stack(out).astype(jnp.int32)


def _attn_kernel(q0, k0, v0, q1, k1, v1, q2, k2, v2, bucket_ref, tab_ref, o_ref, oacc_ref, lacc_ref):
    blk = ATTN_BLOCK
    hh = pl.program_id(1)
    seq = oacc_ref.shape[0]
    refs = ((q0, k0, v0), (q1, k1, v1), (q2, k2, v2))

    def softmax_out(s, v):
        m = jnp.max(s, axis=-1, keepdims=True)
        e = jnp.exp(s - m)
        l = jnp.sum(e, axis=-1, keepdims=True)
        o = _dot(e.astype(BF16), v) / l
        return o, jnp.broadcast_to(m + jnp.log(l), (blk, LANES))

    for g, (_, dilation) in enumerate(DILATED_GROUPS):
        q_ref, k_ref, v_ref = refs[g]
        n_blocks = seq // (dilation * blk)
        bucket = bucket_ref[g]
        head = g * ATTN_HEADS_PER_GROUP + hh
        bias = jnp.where(bucket < 0, MASK_VALUE, 0.0)
        for b in range(N_BUCKETS):
            bias = jnp.where(bucket == b, tab_ref[b, head], bias)

        def merge_store(tok_rows, o, lse, g=g):
            if g == 0:
                oacc_ref[tok_rows, :] = o
                lacc_ref[tok_rows, :] = lse
            else:
                o_old = oacc_ref[tok_rows, :]
                l_old = lacc_ref[tok_rows, :]
                m = jnp.maximum(l_old, lse)
                w_old = jnp.exp(l_old - m)
                w_new = jnp.exp(lse - m)
                den = w_old + w_new
                oacc_ref[tok_rows, :] = (w_old * o_old + w_new * o) / den
                lacc_ref[tok_rows, :] = m + jnp.log(den)

        def tok_rows(off, r, dilation=dilation):
            if dilation == 1:
                return pl.ds(off, blk)
            return pl.ds(off * dilation + r, blk, stride=dilation)

        def unit(off, r, q_ref=q_ref, k_ref=k_ref, v_ref=v_ref, bias=bias, merge_store=merge_store,
                 tok_rows=tok_rows):
            cs = slice(r * LANES, (r + 1) * LANES)
            s = _dot_nt(q_ref[pl.ds(off, blk), cs], k_ref[pl.ds(off - blk, 2 * blk), cs]) + bias
            o, lse = softmax_out(s, v_ref[pl.ds(off - blk, 2 * blk), cs])
            merge_store(tok_rows(off, r), o, lse)

        for r in range(dilation):
            cs = slice(r * LANES, (r + 1) * LANES)
            s0 = _dot_nt(q_ref[0:blk, cs], k_ref[0:blk, cs]) + bias[:, blk:]
            o0, lse0 = softmax_out(s0, v_ref[0:blk, cs])
            merge_store(tok_rows(0, r), o0, lse0)

        ways = ATTN_UNITS_PER_STEP
        if dilation == 1:
            part = n_blocks // ways
            for w in range(1, ways):
                unit(w * part * blk, 0)

            def body(n, carry, unit=unit, part=part):
                off = pl.multiple_of(n * blk, blk)
                for w in range(ways):
                    unit(off + w * part * blk, 0)
                return carry

            lax.fori_loop(1, part, body, 0)
        else:
            for r in range(0, dilation, ways):
                def body(n, carry, unit=unit, r=r):
                    off = pl.multiple_of(n * blk, blk)
                    for w in range(ways):
                        unit(off, r + w)
                    return carry

                lax.fori_loop(1, n_blocks, body, 0)

    o_ref[...] = oacc_ref[...].astype(o_ref.dtype)


def _dilated_attention(qkv_groups, rel_bias, batch, seq):
    hg = ATTN_HEADS_PER_GROUP
    in_specs, operands = [], []
    for (_, dilation), arr in zip(DILATED_GROUPS, qkv_groups):
        for section in range(3):
            in_specs.append(pl.BlockSpec((None, None, seq // dilation, dilation * LANES),
                                         lambda b, h, section=section: (b, section * hg + h, 0, 0)))
            operands.append(arr)
    buckets = _band_buckets()
    in_specs.append(pl.BlockSpec(buckets.shape, lambda b, h: (0, 0, 0)))
    in_specs.append(pl.BlockSpec(memory_space=pltpu.SMEM))
    return pl.pallas_call(
        _attn_kernel,
        grid=(batch, hg),
        in_specs=in_specs,
        out_specs=pl.BlockSpec((seq, LANES), lambda b, h: (b, h)),
        out_shape=jax.ShapeDtypeStruct((batch * seq, hg * LANES), BF16),
        scratch_shapes=[pltpu.VMEM((seq, LANES), F32), pltpu.VMEM((seq, LANES), F32)],
        compiler_params=_cparams(("parallel", "parallel")),
        name="dilated_attn",
    )(*operands, buckets, rel_bias.astype(F32))


_STAGE_BF16 = ("at", "rt", "bt", "kt", "bh", "kh", "vb")
_STAGE_F32 = ("egc", "bonus", "gate")
_STAGE_NAMES = _STAGE_BF16 + _STAGE_F32
_N_HANDOFF = 9


def _seg_sum(x, seg):
    w = seg.shape[0]
    return jnp.concatenate([_dot_split2(x[:, i * w:(i + 1) * w], seg) for i in range(x.shape[1] // w)], axis=1)


def _rwkv_prepare(r_ref, k_ref, v_ref, xl_ref, wl_ref, w0_ref, a0_ref, kk_ref, ka_ref, rk_ref, seg_ref, tri_ref,
                  stage, lora_dims):
    tblk, cw = r_ref.shape
    ch = RWKV_CHUNK
    r, k, v, xl = r_ref[...], k_ref[...], v_ref[...], xl_ref[...]

    d_w, d_a, _ = lora_dims
    lane_l = lax.broadcasted_iota(jnp.int32, xl.shape, 1)
    act = jnp.where(lane_l < d_w, jnp.tanh(xl), jnp.where(lane_l < d_w + d_a, xl, _sigmoid(xl)))
    up = _dot(act.astype(BF16), wl_ref[...])
    w_arg = -(w0_ref[...] + up[:, 0:cw])
    softplus = jnp.maximum(w_arg, 0.0) + jnp.log1p(jnp.exp(-jnp.abs(w_arg)))
    logdecay = -jnp.exp(-softplus - 0.5)
    a_gate = _sigmoid(a0_ref[...] + up[:, cw:2 * cw])
    stage["gate"][...] = up[:, 2 * cw:3 * cw]
    yield

    seg = seg_ref[...]
    kk = k * kk_ref[...]
    kk = kk * lax.rsqrt(jnp.maximum(_seg_sum(kk * kk, seg), 1e-24))
    k2 = k * (1.0 + (a_gate - 1.0) * ka_ref[...])
    a_vec = -kk
    b_vec = kk * a_gate
    stage["bonus"][...] = _seg_sum(r * k2 * rk_ref[...], seg) * v
    yield

    tri = tri_ref[...]
    for c in range(tblk // ch):
        rows = slice(c * ch, (c + 1) * ch)
        ld = logdecay[rows]
        g_inc = _dot_split3_left(tri, ld)
        g_exc = g_inc - ld
        e_neg = jnp.exp(-g_inc)
        e_last = jnp.exp(g_inc[ch - 1:ch, :])
        bt = b_vec[rows] * e_neg
        kt = k2[rows] * e_neg
        stage["at"][rows, :] = (a_vec[rows] * jnp.exp(g_exc)).astype(BF16)
        stage["rt"][rows, :] = (r[rows] * jnp.exp(g_inc)).astype(BF16)
        stage["bt"][rows, :] = bt.astype(BF16)
        stage["kt"][rows, :] = kt.astype(BF16)
        stage["bh"][rows, :] = (bt * e_last).astype(BF16)
        stage["kh"][rows, :] = (kt * e_last).astype(BF16)
        stage["egc"][c] = jnp.broadcast_to(e_last, stage["egc"].shape[1:])
        yield
    stage["vb"][...] = v.astype(BF16)


def _rwkv_solve(stage, state_ref, y_ref, o_ref, gw_ref, gb_ref, seg_ref, handoff, zero_off):
    tblk, cw = o_ref.shape
    ch = RWKV_CHUNK
    n_chunks = tblk // ch
    n_pairs = cw // LANES
    hd = RWKV_HEAD_DIM

    lane = lax.broadcasted_iota(jnp.int32, (1, LANES), 1)
    m0, m1 = lane < hd, lane >= hd
    lane2 = lax.broadcasted_iota(jnp.int32, (1, 2 * LANES), 1) % LANES
    m0w, m1w = lane2 < hd, lane2 >= hd
    rr = lax.broadcasted_iota(jnp.int32, (ch, 2 * LANES), 0)
    cc = lax.broadcasted_iota(jnp.int32, (ch, 2 * LANES), 1) % ch
    strict_lower = cc < rr
    incl_lower = cc <= rr
    eye2 = (lax.broadcasted_iota(jnp.int32, (ch, LANES), 0)
            == lax.broadcasted_iota(jnp.int32, (ch, LANES), 1) % ch).astype(F32)
    bd_r = lax.broadcasted_iota(jnp.int32, (LANES, LANES), 0) >= hd
    bd_c = lax.broadcasted_iota(jnp.int32, (LANES, LANES), 1) >= hd
    block_diag = bd_r == bd_c
    zeros_cv = jnp.zeros((ch, LANES), BF16)

    def keep(mask, x):
        return jnp.where(mask, x, jnp.zeros_like(x))

    problems = [(c, p) for c in range(n_chunks) for p in range(n_pairs)]

    def blk(name, cp):
        c, p = cp
        return stage[name][c * ch:(c + 1) * ch, p * LANES:(p + 1) * LANES]

    aakv_s, tcat_s, g2_s, wu_s, ry_s, rt2_s, y0_s, mlr_s, n0_s = handoff

    g1, pw, tinv = {}, {}, {}
    for i, cp in enumerate(problems):
        bt, kt = blk("bt", cp), blk("kt", cp)
        x = jnp.concatenate([blk("at", cp), blk("rt", cp)], axis=0)
        ym = jnp.concatenate([keep(m0, bt), keep(m0, kt), keep(m1, kt), keep(m1, bt)], axis=0)
        g_all = _dot_nt(x, ym)
        g1[cp] = jnp.where(strict_lower, g_all[:ch], 0.0)
        g2_s[i] = jnp.where(incl_lower, g_all[ch:], 0.0).astype(BF16)
        pw[cp] = jnp.where(m0, g1[cp][:, :LANES], g1[cp][:, LANES:])
        tinv[cp] = eye2 + pw[cp]
    yield

    def block_diag_of(cat):
        cat = cat.astype(BF16)
        return jnp.concatenate([keep(m0, cat), keep(m1, cat)], axis=0)

    n_sq = int(math.log2(ch)) - 1
    for cp in problems:
        pw[cp] = _dot(pw[cp].astype(BF16), block_diag_of(pw[cp]))
    yield
    for _ in range(n_sq - 1):
        for cp in problems:
            both = _dot(jnp.concatenate([tinv[cp], pw[cp]], axis=0).astype(BF16), block_diag_of(pw[cp]))
            tinv[cp] = tinv[cp] + both[:ch]
            pw[cp] = both[ch:]
        yield
    for cp in problems:
        tinv[cp] = tinv[cp] + _dot(tinv[cp].astype(BF16), block_diag_of(pw[cp]))
    yield
    for i, cp in enumerate(problems):
        vb = blk("vb", cp)
        v_stack = jnp.concatenate([zeros_cv, keep(m0, vb), keep(m1, vb), zeros_cv], axis=0)
        aakv_s[i] = _dot(g1[cp].astype(BF16), v_stack).astype(BF16)
        tcat_s[i] = tinv[cp].astype(BF16)
    yield
    for i, cp in enumerate(problems):
        j = zero_off + i
        rhs = jnp.concatenate([blk("at", cp), aakv_s[j]], axis=1)
        wu_s[i] = _dot(tcat_s[j], jnp.concatenate([keep(m0w, rhs), keep(m1w, rhs)], axis=0))
    yield
    for i, cp in enumerate(problems):
        j = zero_off + i
        upper = wu_s[j].astype(BF16)
        lower = jnp.concatenate([zeros_cv, blk("vb", cp)], axis=1)
        rhs = jnp.concatenate([keep(m0w, upper), keep(m0w, lower), keep(m1w, lower), keep(m1w, upper)], axis=0)
        ry_s[i] = _dot(g2_s[j], rhs)
    yield
    for i, cp in enumerate(problems):
        j = zero_off + i
        wu, ry = wu_s[j], ry_s[j]
        bkh = jnp.concatenate([blk("bh", cp), blk("kh", cp)], axis=0)
        rt2_s[i] = (blk("rt", cp).astype(F32) + ry[:, :LANES]).astype(BF16)
        y0_s[i] = ry[:, LANES:]
        wt_t = wu[:, :LANES].T.astype(BF16)
        mlr_s[i] = jnp.where(block_diag, _dot(wt_t, bkh[:ch]), 0.0).astype(BF16)
        uv_t = jnp.concatenate([wu[:, LANES:], blk("vb", cp).astype(F32)], axis=0).T.astype(BF16)
        n0_s[i] = jnp.where(block_diag, _dot(uv_t, bkh), 0.0)
    yield

    for c in range(n_chunks):
        rows = slice(c * ch, (c + 1) * ch)
        for p in range(n_pairs):
            cs = slice(p * LANES, (p + 1) * LANES)
            j = zero_off + (c * n_pairs + p)
            ht = state_ref[p]
            htb = ht.astype(BF16)
            y_ref[rows, cs] = _dot_nt(rt2_s[j], htb) + y0_s[j]
            state_ref[p] = ht * stage["egc"][c, 0:1, cs] + _dot(htb, mlr_s[j]) + n0_s[j]
    yield

    seg = seg_ref[...]
    y = y_ref[...]
    inv_hd = 1.0 / hd
    mu = _seg_sum(y, seg) * inv_hd
    yc = y - mu
    var = _seg_sum(yc * yc, seg) * inv_hd
    yn = yc * lax.rsqrt(var + GN_EPS) * gw_ref[...] + gb_ref[...]
    o_ref[...] = ((yn + stage["bonus"][...]) * stage["gate"][...]).astype(o_ref.dtype)


def _interleave(streams, order):
    for i in order:
        next(streams[i], None)
    for g in streams:
        for _ in g:
            pass


_RWKV_TRACE_ORDER = (0, 1, 0, 0, 1, 0, 0, 0, 0, 0, 1, 0, 1, 0, 1, 0, 1, 0, 0)


def _rwkv_kernel(*refs, lora_dims):
    n_in = 15
    (r_ref, k_ref, v_ref, xl_ref, wl_ref, w0_ref, a0_ref, kk_ref, ka_ref, rk_ref, gw_ref, gb_ref, seg_ref,
     tri_ref, zero_ref) = refs[:n_in]
    o_ref = refs[n_in]
    scratch = refs[n_in + 1:]
    state_ref, y_ref = scratch[:2]
    handoff = scratch[2:2 + _N_HANDOFF]
    ns = len(_STAGE_NAMES)
    stages = [dict(zip(_STAGE_NAMES, scratch[2 + _N_HANDOFF + i * ns: 2 + _N_HANDOFF + (i + 1) * ns]))
              for i in range(2)]
    t = pl.program_id(2)

    @pl.when(t == 0)
    def _():
        state_ref[...] = jnp.zeros_like(state_ref)
        for ref in stages[1].values():
            ref[...] = jnp.zeros_like(ref)

    for parity in range(2):
        @pl.when(t % 2 == parity)
        def _(parity=parity):
            _interleave([
                _rwkv_solve(stages[1 - parity], state_ref, y_ref, o_ref, gw_ref, gb_ref, seg_ref, handoff,
                            zero_ref[0]),
                _rwkv_prepare(r_ref, k_ref, v_ref, xl_ref, wl_ref, w0_ref, a0_ref, kk_ref, ka_ref, rk_ref,
                              seg_ref, tri_ref, stages[parity], lora_dims)], _RWKV_TRACE_ORDER)


def _rwkv7(zr, zl, w0, w_decay_up, a0, w_aaa_up, w_gate_up, k_k, k_a, r_k, gn_w, gn_b, batch, seq):
    tok = zr.shape[0]
    width = zr.shape[1] // 3
    lw = zl.shape[1]
    cw = RWKV_HPB * RWKV_HEAD_DIM
    nhb = width // cw
    tblk = RWKV_TBLK
    nt = seq // tblk
    ch = RWKV_CHUNK

    def pad_rows(w, start):
        return jnp.pad(w, ((start, lw - start - w.shape[0]), (0, 0)))

    wl = jnp.stack([pad_rows(w_decay_up, 0), pad_rows(w_aaa_up, DECAY_LORA),
                    pad_rows(w_gate_up, DECAY_LORA + AAA_LORA)], axis=0)
    wl = wl.reshape(3, lw, nhb, cw).transpose(2, 1, 0, 3).reshape(nhb, lw, 3 * cw).astype(BF16)

    idx = jnp.arange(MXU_DIM) // RWKV_HEAD_DIM
    seg = (idx[:, None] == idx[None, :]).astype(BF16)
    tri = (jnp.arange(ch)[:, None] >= jnp.arange(ch)[None, :]).astype(BF16)

    def in_blk(b, t):
        return b * nt + jnp.minimum(t, nt - 1)

    def col(sec):
        return pl.BlockSpec((tblk, cw), lambda b, h, t: (in_blk(b, t), sec * nhb + h))

    vec = pl.BlockSpec((1, cw), lambda b, h, t: (0, h))
    row_vec = lambda a: a.reshape(1, width)
    stage_shapes = ([pltpu.VMEM((tblk, cw), BF16)] * len(_STAGE_BF16)
                    + [pltpu.VMEM((tblk // ch, 8, cw), F32),
                       pltpu.VMEM((tblk, cw), F32), pltpu.VMEM((tblk, cw), F32)])
    n_prob = (tblk // ch) * (cw // LANES)
    handoff_shapes = [
        pltpu.VMEM((n_prob, ch, LANES), BF16),
        pltpu.VMEM((n_prob, ch, LANES), BF16),
        pltpu.VMEM((n_prob, ch, 2 * LANES), BF16),
        pltpu.VMEM((n_prob, ch, 2 * LANES), F32),
        pltpu.VMEM((n_prob, ch, 2 * LANES), F32),
        pltpu.VMEM((n_prob, ch, LANES), BF16), pltpu.VMEM((n_prob, ch, LANES), F32),
        pltpu.VMEM((n_prob, LANES, LANES), BF16), pltpu.VMEM((n_prob, LANES, LANES), F32),
    ]
    assert len(handoff_shapes) == _N_HANDOFF
    kern = functools.partial(_rwkv_kernel, lora_dims=(DECAY_LORA, AAA_LORA, GATE_LORA))
    return pl.pallas_call(
        kern,
        grid=(batch, nhb, nt + 1),
        in_specs=[
            col(0), col(1), col(2),
            pl.BlockSpec((tblk, lw), lambda b, h, t: (in_blk(b, t), 0)),
            pl.BlockSpec((None, lw, 3 * cw), lambda b, h, t: (h, 0, 0)),
            vec, vec, vec, vec, vec, vec, vec,
            pl.BlockSpec((MXU_DIM, MXU_DIM), lambda b, h, t: (0, 0)),
            pl.BlockSpec((ch, ch), lambda b, h, t: (0, 0)),
            pl.BlockSpec(memory_space=pltpu.SMEM),
        ],
        out_specs=pl.BlockSpec((tblk, cw), lambda b, h, t: (b * nt + jnp.maximum(t - 1, 0), h)),
        out_shape=jax.ShapeDtypeStruct((tok, width), BF16),
        scratch_shapes=[
            pltpu.VMEM((cw // LANES, LANES, LANES), F32),
            pltpu.VMEM((tblk, cw), F32),
        ] + handoff_shapes + stage_shapes * 2,
        compiler_params=_cparams(("parallel", "parallel", "arbitrary")),
        name="rwkv7_chunked",
    )(zr, zr, zr, zl, wl,
      row_vec(w0), row_vec(a0), row_vec(k_k), row_vec(k_a), row_vec(r_k), row_vec(gn_w), row_vec(gn_b),
      seg, tri, jnp.zeros((1,), jnp.int32))


def _merge_kernel(attn_ref, rw_ref, g0_ref, g1_ref, wa_ref, wr_ref, out_ref):
    attn_d = _dot(attn_ref[...], wa_ref[...].astype(BF16))
    rwkv_d = _dot(rw_ref[...], wr_ref[...].astype(BF16))
    out_ref[...] = (g0_ref[...].astype(F32) * attn_d + g1_ref[...].astype(F32) * rwkv_d).astype(out_ref.dtype)


def _merge(attn, rw, gates, w_attn_up, w_rwkv_up, tm=1024, tn=512):
    tok = rw.shape[0]
    d = w_attn_up.shape[1]
    nj = d // tn
    return pl.pallas_call(
        _merge_kernel,
        grid=(tok // tm, nj),
        in_specs=[
            pl.BlockSpec((tm, attn.shape[1]), lambda i, j: (i, 0)),
            pl.BlockSpec((tm, rw.shape[1]), lambda i, j: (i, 0)),
            pl.BlockSpec((tm, tn), lambda i, j: (i, j)),
            pl.BlockSpec((tm, tn), lambda i, j: (i, nj + j)),
            pl.BlockSpec((w_attn_up.shape[0], tn), lambda i, j: (0, j)),
            pl.BlockSpec((w_rwkv_up.shape[0], tn), lambda i, j: (0, j)),
        ],
        out_specs=pl.BlockSpec((tm, tn), lambda i, j: (i, j)),
        out_shape=jax.ShapeDtypeStruct((tok, d), BF16),
        compiler_params=_cparams(("parallel", "arbitrary")),
        name="merge_gated_up",
    )(attn, rw, gates, gates, w_attn_up, w_rwkv_up)


def _ple_kernel(h_ref, wg_ref, p_ref, wp_ref, x_ref, o_ref):
    gate = _sigmoid(_dot(h_ref[...], wg_ref[...].astype(BF16)))
    o_ref[...] = x_ref[...] + gate * _dot(p_ref[...], wp_ref[...].astype(BF16))


def _ple(hn, w_gate, p, w_proj, x, tm=1024, tn=512):
    m, k = hn.shape
    n = w_gate.shape[1]
    kp = p.shape[1]
    return pl.pallas_call(
        _ple_kernel,
        grid=(m // tm, n // tn),
        in_specs=[
            pl.BlockSpec((tm, k), lambda i, j: (i, 0)),
            pl.BlockSpec((k, tn), lambda i, j: (0, j)),
            pl.BlockSpec((tm, kp), lambda i, j: (i, 0)),
            pl.BlockSpec((kp, tn), lambda i, j: (0, j)),
            pl.BlockSpec((tm, tn), lambda i, j: (i, j)),
        ],
        out_specs=pl.BlockSpec((tm, tn), lambda i, j: (i, j)),
        out_shape=jax.ShapeDtypeStruct((m, n), F32),
        compiler_params=_cparams(("parallel", "arbitrary")),
        name="ple_gated",
    )(hn, w_gate, p, w_proj, x)


def _layer(x2d, p2d, batch, seq, norm_mix, w_in, q_gain, k_gain, rel_bias, w_attn_up, shift_mix, w0,
           w_decay_up, a0, w_aaa_up, w_gate_up, k_k, k_a, r_k, gn_w, gn_b, w_rwkv_up, w_out, norm_mlp,
           w_mlp_in, w_mlp_out, norm_ple, w_ple_gate, w_ple_proj):
    n_attn_heads = len(DILATED_GROUPS) * ATTN_HEADS_PER_GROUP
    attn_width = n_attn_heads * ATTN_HEAD_DIM
    rwkv_width = w_rwkv_up.shape[0]
    a_end = 3 * attn_width
    z_end = a_end + 3 * rwkv_width
    r_end = z_end + DECAY_LORA + AAA_LORA + GATE_LORA

    h = _rmsnorm_bf16(x2d, norm_mix)

    gains = jnp.stack([q_gain * (ATTN_HEAD_DIM ** -0.5), k_gain, jnp.ones_like(q_gain)]).reshape(3, 1, LANES)
    w_in_t = jnp.swapaxes(w_in, 0, 1)
    tn = ATTN_HEADS_PER_GROUP * ATTN_HEAD_DIM
    qkv_groups = [_qkv_proj(h, w_in_t, gains, gi, dilation, batch, seq)
                  for gi, (_, dilation) in enumerate(DILATED_GROUPS)]
    mix_z = shift_mix[: z_end - a_end].reshape(1, -1)
    zr = _matmul_token_shift(h, w_in_t, mix_z, seq, n=z_end - a_end, wt_row_blk=a_end // tn, tn=tn,
                             name="w_in_rkv")
    mix_l = jnp.pad(shift_mix[z_end - a_end:], (0, tn - (r_end - z_end))).reshape(1, tn)
    zl = _matmul_token_shift(h, w_in_t, mix_l, seq, n=tn, wt_row_blk=z_end // tn, tn=tn, name="w_in_lora")
    gates = _matmul(h, w_in_t, n=w_in.shape[1] - r_end, wt_row0=r_end, act="sigmoid", out_dtype=BF16,
                    name="w_in_gates")

    attn = _dilated_attention(qkv_groups, rel_bias, batch, seq)

    rw = _rwkv7(zr, zl, w0, w_decay_up, a0, w_aaa_up, w_gate_up, k_k, k_a, r_k, gn_w, gn_b, batch, seq)

    merged = _merge(attn, rw, gates, w_attn_up, w_rwkv_up)
    x1 = _matmul_residual(merged, w_out, x2d, name="w_out_res")

    h2 = _rmsnorm_bf16(x1, norm_mlp)
    u = _matmul(h2, w_mlp_in, act="relu2", out_dtype=BF16, name="mlp_in")
    x2 = _matmul_kacc_residual(u, w_mlp_out, x1, name="mlp_out_res")

    hn = _rmsnorm_bf16(x2, norm_ple)
    return _ple(hn, w_ple_gate, p2d.astype(BF16), w_ple_proj, x2)


def kernel(x, p, norm_mix, w_in, q_gain, k_gain, rel_bias, w_attn_up, shift_mix, w0, w_decay_up, a0, w_aaa_up,
           w_gate_up, k_k, k_a, r_k, gn_w, gn_b, w_rwkv_up, w_out, norm_mlp, w_mlp_in, w_mlp_out, norm_ple,
           w_ple_gate, w_ple_proj):
    batch, seq, d_model = x.shape
    depth = p.shape[0]
    x2d = x.reshape(batch * seq, d_model)
    for i in range(depth):
        x2d = _layer(x2d, p[i].reshape(batch * seq, -1), batch, seq, norm_mix[i], w_in[i], q_gain[i], k_gain[i],
                     rel_bias, w_attn_up[i], shift_mix[i], w0[i], w_decay_up[i], a0[i], w_aaa_up[i],
                     w_gate_up[i], k_k[i], k_a[i], r_k[i], gn_w[i], gn_b[i], w_rwkv_up[i], w_out[i],
                     norm_mlp[i], w_mlp_in[i], w_mlp_out[i], norm_ple[i], w_ple_gate[i], w_ple_proj[i])
    return x2d.reshape(batch, seq, d_model)
```

```python
import functools
import math

import jax
import jax.numpy as jnp
from jax import lax
from jax.experimental import pallas as pl
from jax.experimental.pallas import tpu as pltpu

F32 = jnp.float32
BF16 = jnp.bfloat16

LANES = 128
SUBLANES = 8
MXU_DIM = 256
ATTN_HEAD_DIM = 128
ATTN_HEADS_PER_GROUP = 4
DILATED_GROUPS = ((128, 1), (512, 4), (2048, 16))
ATTN_BLOCK = 128
ATTN_UNITS_PER_STEP = 4
N_BUCKETS = 32
MAX_DISTANCE = 2048
RWKV_HEAD_DIM = 64
DECAY_LORA = 96
AAA_LORA = 96
GATE_LORA = 256
RMS_EPS = 1e-6
GN_EPS = 64e-5
MASK_VALUE = -1e30

RWKV_CHUNK = 64
RWKV_TBLK = 256
RWKV_HPB = 16

VMEM_LIMIT = 56 * 1024 * 1024


def _cparams(sem):
    return pltpu.CompilerParams(dimension_semantics=sem, vmem_limit_bytes=VMEM_LIMIT)


def _dot(a, b):
    return jnp.dot(a, b, preferred_element_type=F32)


def _dot_nt(a, b):
    return lax.dot_general(a, b, (((1,), (1,)), ((), ())), preferred_element_type=F32)


def _sigmoid(x):
    return 0.5 * jnp.tanh(0.5 * x) + 0.5


def _dot_split2(x, w_bf16):
    hi = x.astype(BF16)
    lo = (x - hi.astype(F32)).astype(BF16)
    return _dot(hi, w_bf16) + _dot(lo, w_bf16)


def _dot_split3_left(w_bf16, x):
    hi = x.astype(BF16)
    r1 = x - hi.astype(F32)
    mid = r1.astype(BF16)
    lo = (r1 - mid.astype(F32)).astype(BF16)
    return _dot(w_bf16, hi) + _dot(w_bf16, mid) + _dot(w_bf16, lo)


def _rmsnorm_kernel(x_ref, g_ref, o_ref):
    x = x_ref[...]
    ms = jnp.mean(x * x, axis=-1, keepdims=True)
    o_ref[...] = (x * lax.rsqrt(ms + RMS_EPS) * g_ref[...]).astype(o_ref.dtype)


def _rmsnorm_bf16(x, gain, tr=512):
    m, d = x.shape
    return pl.pallas_call(
        _rmsnorm_kernel,
        grid=(m // tr,),
        in_specs=[pl.BlockSpec((tr, d), lambda i: (i, 0)), pl.BlockSpec((1, d), lambda i: (0, 0))],
        out_specs=pl.BlockSpec((tr, d), lambda i: (i, 0)),
        out_shape=jax.ShapeDtypeStruct((m, d), BF16),
        compiler_params=_cparams(("parallel",)),
        name="rmsnorm",
    )(x, gain.reshape(1, d))


def _qkv_kernel(a_ref, w_ref, g_ref, o_ref, scr_ref, *, dilation):
    j = pl.program_id(1)
    acc = _dot_nt(a_ref[...], w_ref[...].astype(BF16))
    rows = acc.shape[0] // dilation
    for h in range(ATTN_HEADS_PER_GROUP):
        y = acc[:, h * LANES:(h + 1) * LANES]
        ms = jnp.mean(y * y, axis=-1, keepdims=True)
        y = y * jnp.where(j < 2, lax.rsqrt(ms + RMS_EPS), 1.0) * g_ref[...]
        if dilation == 1:
            o_ref[h] = y.astype(o_ref.dtype)
        else:
            scr_ref[h] = y
            for r in range(dilation):
                o_ref[h, :, r * LANES:(r + 1) * LANES] = (
                    scr_ref[h, pl.ds(r, rows, stride=dilation), :].astype(o_ref.dtype))


def _qkv_proj(h, w_in_t, gains, group, dilation, batch, seq, tm=1024):
    m, d = h.shape
    tn = ATTN_HEADS_PER_GROUP * LANES
    mt_per_b = seq // tm
    hg = ATTN_HEADS_PER_GROUP
    n_groups = len(DILATED_GROUPS)
    return pl.pallas_call(
        functools.partial(_qkv_kernel, dilation=dilation),
        grid=(m // tm, 3),
        in_specs=[
            pl.BlockSpec((tm, d), lambda i, j: (i, 0)),
            pl.BlockSpec((tn, d), lambda i, j: (j * n_groups + group, 0)),
            pl.BlockSpec((None, 1, LANES), lambda i, j: (j, 0, 0)),
        ],
        out_specs=pl.BlockSpec((None, hg, tm // dilation, dilation * LANES),
                               lambda i, j: (i // mt_per_b, j, i % mt_per_b, 0)),
        out_shape=jax.ShapeDtypeStruct((batch, 3 * hg, seq // dilation, dilation * LANES), BF16),
        scratch_shapes=[pltpu.VMEM((hg, tm, LANES), F32)],
        compiler_params=_cparams(("parallel", "arbitrary")),
        name=f"w_in_qkv_d{dilation}",
    )(h, w_in_t, gains)


def _mm_kernel(a_ref, w_ref, o_ref, *, act, transposed_w):
    w = w_ref[...].astype(BF16)
    acc = _dot_nt(a_ref[...], w) if transposed_w else _dot(a_ref[...], w)
    if act == "sigmoid":
        acc = _sigmoid(acc)
    elif act == "relu2":
        acc = jnp.square(jnp.maximum(acc, 0.0))
    o_ref[...] = acc.astype(o_ref.dtype)


def _matmul(a, w, *, n=None, wt_row0=None, act=None, out_dtype=F32, tm=1024, tn=512, name="matmul"):
    m, k = a.shape
    transposed_w = wt_row0 is not None
    n = w.shape[1] if n is None else n
    if transposed_w:
        assert wt_row0 % SUBLANES == 0 and tn % SUBLANES == 0
        w_spec = pl.BlockSpec((pl.Element(tn), pl.Element(k)),
                              lambda i, j: (pl.multiple_of(wt_row0 + j * tn, SUBLANES), 0))
    else:
        w_spec = pl.BlockSpec((k, tn), lambda i, j: (0, j))
    return pl.pallas_call(
        functools.partial(_mm_kernel, act=act, transposed_w=transposed_w),
        grid=(m // tm, n // tn),
        in_specs=[pl.BlockSpec((tm, k), lambda i, j: (i, 0)), w_spec],
        out_specs=pl.BlockSpec((tm, tn), lambda i, j: (i, j)),
        out_shape=jax.ShapeDtypeStruct((m, n), out_dtype),
        compiler_params=_cparams(("parallel", "arbitrary")),
        name=name,
    )(a, w)


def _mm_shift_kernel(a_ref, w_ref, mix_ref, o_ref, carry_ref, *, m_tiles_per_seq):
    i = pl.program_id(0)
    j = pl.program_id(1)

    @pl.when(i == 0)
    def _():
        carry_ref[j] = jnp.zeros(carry_ref.shape[1:], F32)

    z = _dot_nt(a_ref[...], w_ref[...].astype(BF16))
    tm = z.shape[0]
    row = lax.broadcasted_iota(jnp.int32, (tm, 1), 0)
    sublanes = carry_ref.shape[1]
    prev_row = jnp.where(i % m_tiles_per_seq == 0, 0.0, carry_ref[j, sublanes - 1:sublanes, :])
    prev = jnp.where(row == 0, prev_row, pltpu.roll(z, 1, axis=0))
    o_ref[...] = z + mix_ref[...] * (prev - z)
    carry_ref[j] = z[tm - sublanes:tm, :]


def _matmul_token_shift(a, wt, mix, seq, *, n, wt_row_blk, tm=1024, tn=512, name="matmul_shift"):
    m, k = a.shape
    n_tiles = n // tn
    return pl.pallas_call(
        functools.partial(_mm_shift_kernel, m_tiles_per_seq=seq // tm),
        grid=(m // tm, n_tiles),
        in_specs=[pl.BlockSpec((tm, k), lambda i, j: (i, 0)),
                  pl.BlockSpec((tn, k), lambda i, j: (wt_row_blk + j, 0)),
                  pl.BlockSpec((1, tn), lambda i, j: (0, j))],
        out_specs=pl.BlockSpec((tm, tn), lambda i, j: (i, j)),
        out_shape=jax.ShapeDtypeStruct((m, n), F32),
        scratch_shapes=[pltpu.VMEM((n_tiles, SUBLANES, tn), F32)],
        compiler_params=_cparams(("arbitrary", "arbitrary")),
        name=name,
    )(a, wt, mix)


def _mm_res_kernel(a_ref, w_ref, r_ref, o_ref):
    o_ref[...] = r_ref[...] + _dot(a_ref[...], w_ref[...].astype(BF16))


def _matmul_residual(a, w, res, *, tm=1024, tn=512, name="matmul_res"):
    m, k = a.shape
    n = w.shape[1]
    return pl.pallas_call(
        _mm_res_kernel,
        grid=(m // tm, n // tn),
        in_specs=[
            pl.BlockSpec((tm, k), lambda i, j: (i, 0)),
            pl.BlockSpec((k, tn), lambda i, j: (0, j)),
            pl.BlockSpec((tm, tn), lambda i, j: (i, j)),
        ],
        out_specs=pl.BlockSpec((tm, tn), lambda i, j: (i, j)),
        out_shape=jax.ShapeDtypeStruct((m, n), F32),
        compiler_params=_cparams(("parallel", "arbitrary")),
        name=name,
    )(a, w, res)


def _mm_kacc_res_kernel(a_ref, w_ref, r_ref, o_ref):
    k = pl.program_id(2)

    @pl.when(k == 0)
    def _():
        o_ref[...] = r_ref[...] + _dot(a_ref[...], w_ref[...].astype(BF16))

    @pl.when(k > 0)
    def _():
        o_ref[...] += _dot(a_ref[...], w_ref[...].astype(BF16))


def _matmul_kacc_residual(a, w, res, *, tm=1024, tn=1024, tk=2048, name="matmul_kacc"):
    m, kdim = a.shape
    n = w.shape[1]
    return pl.pallas_call(
        _mm_kacc_res_kernel,
        grid=(m // tm, n // tn, kdim // tk),
        in_specs=[
            pl.BlockSpec((tm, tk), lambda i, j, k: (i, k)),
            pl.BlockSpec((tk, tn), lambda i, j, k: (k, j)),
            pl.BlockSpec((tm, tn), lambda i, j, k: (i, j)),
        ],
        out_specs=pl.BlockSpec((tm, tn), lambda i, j, k: (i, j)),
        out_shape=jax.ShapeDtypeStruct((m, n), F32),
        compiler_params=_cparams(("parallel", "parallel", "arbitrary")),
        name=name,
    )(a, w, res)


def _t5_bucket(dist):
    max_exact = N_BUCKETS // 2
    d_f = jnp.maximum(dist, 1).astype(F32)
    large = max_exact + (jnp.log(d_f / max_exact) / math.log(MAX_DISTANCE / max_exact)
                         * (N_BUCKETS - max_exact)).astype(jnp.int32)
    large = jnp.minimum(large, N_BUCKETS - 1)
    return jnp.where(dist < max_exact, dist, large)


def _band_buckets():
    blk = ATTN_BLOCK
    rel = (blk + jnp.arange(blk))[:, None] - jnp.arange(2 * blk)[None, :]
    out = []
    for window, dilation in DILATED_GROUPS:
        band = (rel >= 0) & (rel <= window // dilation)
        out.append(jnp.where(band, _t5_bucket(jnp.maximum(rel, 0) * dilation), -1))
    return jnp.stack(out).astype(jnp.int32)


def _attn_kernel(q0, k0, v0, q1, k1, v1, q2, k2, v2, bucket_ref, tab_ref, o_ref, oacc_ref, lacc_ref):
    blk = ATTN_BLOCK
    hh = pl.program_id(1)
    seq = oacc_ref.shape[0]
    refs = ((q0, k0, v0), (q1, k1, v1), (q2, k2, v2))

    def softmax_out(s, v):
        m = jnp.max(s, axis=-1, keepdims=True)
        e = jnp.exp(s - m)
        l = jnp.sum(e, axis=-1, keepdims=True)
        o = _dot(e.astype(BF16), v) / l
        return o, jnp.broadcast_to(m + jnp.log(l), (blk, LANES))

    for g, (_, dilation) in enumerate(DILATED_GROUPS):
        q_ref, k_ref, v_ref = refs[g]
        n_blocks = seq // (dilation * blk)
        bucket = bucket_ref[g]
        head = g * ATTN_HEADS_PER_GROUP + hh
        bias = jnp.where(bucket < 0, MASK_VALUE, 0.0)
        for b in range(N_BUCKETS):
            bias = jnp.where(bucket == b, tab_ref[b, head], bias)

        def merge_store(tok_rows, o, lse, g=g):
            if g == 0:
                oacc_ref[tok_rows, :] = o
                lacc_ref[tok_rows, :] = lse
            else:
                o_old = oacc_ref[tok_rows, :]
                l_old = lacc_ref[tok_rows, :]
                m = jnp.maximum(l_old, lse)
                w_old = jnp.exp(l_old - m)
                w_new = jnp.exp(lse - m)
                den = w_old + w_new
                oacc_ref[tok_rows, :] = (w_old * o_old + w_new * o) / den
                lacc_ref[tok_rows, :] = m + jnp.log(den)

        def tok_rows(off, r, dilation=dilation):
            if dilation == 1:
                return pl.ds(off, blk)
            return pl.ds(off * dilation + r, blk, stride=dilation)

        def unit(off, r, q_ref=q_ref, k_ref=k_ref, v_ref=v_ref, bias=bias, merge_store=merge_store,
                 tok_rows=tok_rows):
            cs = slice(r * LANES, (r + 1) * LANES)
            s = _dot_nt(q_ref[pl.ds(off, blk), cs], k_ref[pl.ds(off - blk, 2 * blk), cs]) + bias
            o, lse = softmax_out(s, v_ref[pl.ds(off - blk, 2 * blk), cs])
            merge_store(tok_rows(off, r), o, lse)

        for r in range(dilation):
            cs = slice(r * LANES, (r + 1) * LANES)
            s0 = _dot_nt(q_ref[0:blk, cs], k_ref[0:blk, cs]) + bias[:, blk:]
            o0, lse0 = softmax_out(s0, v_ref[0:blk, cs])
            merge_store(tok_rows(0, r), o0, lse0)

        ways = ATTN_UNITS_PER_STEP
        if dilation == 1:
            part = n_blocks // ways
            for w in range(1, ways):
                unit(w * part * blk, 0)

            def body(n, carry, unit=unit, part=part):
                off = pl.multiple_of(n * blk, blk)
                for w in range(ways):
                    unit(off + w * part * blk, 0)
                return carry

            lax.fori_loop(1, part, body, 0)
        else:
            for r in range(0, dilation, ways):
                def body(n, carry, unit=unit, r=r):
                    off = pl.multiple_of(n * blk, blk)
                    for w in range(ways):
                        unit(off, r + w)
                    return carry

                lax.fori_loop(1, n_blocks, body, 0)

    o_ref[...] = oacc_ref[...].astype(o_ref.dtype)


def _dilated_attention(qkv_groups, rel_bias, batch, seq):
    hg = ATTN_HEADS_PER_GROUP
    in_specs, operands = [], []
    for (_, dilation), arr in zip(DILATED_GROUPS, qkv_groups):
        for section in range(3):
            in_specs.append(pl.BlockSpec((None, None, seq // dilation, dilation * LANES),
                                         lambda b, h, section=section: (b, section * hg + h, 0, 0)))
            operands.append(arr)
    buckets = _band_buckets()
    in_specs.append(pl.BlockSpec(buckets.shape, lambda b, h: (0, 0, 0)))
    in_specs.append(pl.BlockSpec(memory_space=pltpu.SMEM))
    return pl.pallas_call(
        _attn_kernel,
        grid=(batch, hg),
        in_specs=in_specs,
        out_specs=pl.BlockSpec((seq, LANES), lambda b, h: (b, h)),
        out_shape=jax.ShapeDtypeStruct((batch * seq, hg * LANES), BF16),
        scratch_shapes=[pltpu.VMEM((seq, LANES), F32), pltpu.VMEM((seq, LANES), F32)],
        compiler_params=_cparams(("parallel", "parallel")),
        name="dilated_attn",
    )(*operands, buckets, rel_bias.astype(F32))


_STAGE_BF16 = ("at", "rt", "bt", "kt", "bh", "kh", "vb")
_STAGE_F32 = ("egc", "bonus", "gate")
_STAGE_NAMES = _STAGE_BF16 + _STAGE_F32
_N_HANDOFF = 9


def _seg_sum(x, seg):
    w = seg.shape[0]
    return jnp.concatenate([_dot_split2(x[:, i * w:(i + 1) * w], seg) for i in range(x.shape[1] // w)], axis=1)


def _rwkv_prepare(r_ref, k_ref, v_ref, xl_ref, wl_ref, w0_ref, a0_ref, kk_ref, ka_ref, rk_ref, seg_ref, tri_ref,
                  stage, lora_dims):
    tblk, cw = r_ref.shape
    ch = RWKV_CHUNK
    r, k, v, xl = r_ref[...], k_ref[...], v_ref[...], xl_ref[...]

    d_w, d_a, _ = lora_dims
    lane_l = lax.broadcasted_iota(jnp.int32, xl.shape, 1)
    act = jnp.where(lane_l < d_w, jnp.tanh(xl), jnp.where(lane_l < d_w + d_a, xl, _sigmoid(xl)))
    up = _dot(act.astype(BF16), wl_ref[...])
    w_arg = -(w0_ref[...] + up[:, 0:cw])
    softplus = jnp.maximum(w_arg, 0.0) + jnp.log1p(jnp.exp(-jnp.abs(w_arg)))
    logdecay = -jnp.exp(-softplus - 0.5)
    a_gate = _sigmoid(a0_ref[...] + up[:, cw:2 * cw])
    stage["gate"][...] = up[:, 2 * cw:3 * cw]
    yield

    seg = seg_ref[...]
    kk = k * kk_ref[...]
    kk = kk * lax.rsqrt(jnp.maximum(_seg_sum(kk * kk, seg), 1e-24))
    k2 = k * (1.0 + (a_gate - 1.0) * ka_ref[...])
    a_vec = -kk
    b_vec = kk * a_gate
    stage["bonus"][...] = _seg_sum(r * k2 * rk_ref[...], seg) * v
    yield

    tri = tri_ref[...]
    for c in range(tblk // ch):
        rows = slice(c * ch, (c + 1) * ch)
        ld = logdecay[rows]
        g_inc = _dot_split3_left(tri, ld)
        g_exc = g_inc - ld
        e_neg = jnp.exp(-g_inc)
        e_last = jnp.exp(g_inc[ch - 1:ch, :])
        bt = b_vec[rows] * e_neg
        kt = k2[rows] * e_neg
        stage["at"][rows, :] = (a_vec[rows] * jnp.exp(g_exc)).astype(BF16)
        stage["rt"][rows, :] = (r[rows] * jnp.exp(g_inc)).astype(BF16)
        stage["bt"][rows, :] = bt.astype(BF16)
        stage["kt"][rows, :] = kt.astype(BF16)
        stage["bh"][rows, :] = (bt * e_last).astype(BF16)
        stage["kh"][rows, :] = (kt * e_last).astype(BF16)
        stage["egc"][c] = jnp.broadcast_to(e_last, stage["egc"].shape[1:])
        yield
    stage["vb"][...] = v.astype(BF16)


def _rwkv_solve(stage, state_ref, y_ref, o_ref, gw_ref, gb_ref, seg_ref, handoff, zero_off):
    tblk, cw = o_ref.shape
    ch = RWKV_CHUNK
    n_chunks = tblk // ch
    n_pairs = cw // LANES
    hd = RWKV_HEAD_DIM

    lane = lax.broadcasted_iota(jnp.int32, (1, LANES), 1)
    m0, m1 = lane < hd, lane >= hd
    lane2 = lax.broadcasted_iota(jnp.int32, (1, 2 * LANES), 1) % LANES
    m0w, m1w = lane2 < hd, lane2 >= hd
    rr = lax.broadcasted_iota(jnp.int32, (ch, 2 * LANES), 0)
    cc = lax.broadcasted_iota(jnp.int32, (ch, 2 * LANES), 1) % ch
    strict_lower = cc < rr
    incl_lower = cc <= rr
    eye2 = (lax.broadcasted_iota(jnp.int32, (ch, LANES), 0)
            == lax.broadcasted_iota(jnp.int32, (ch, LANES), 1) % ch).astype(F32)
    bd_r = lax.broadcasted_iota(jnp.int32, (LANES, LANES), 0) >= hd
    bd_c = lax.broadcasted_iota(jnp.int32, (LANES, LANES), 1) >= hd
    block_diag = bd_r == bd_c
    zeros_cv = jnp.zeros((ch, LANES), BF16)

    def keep(mask, x):
        return jnp.where(mask, x, jnp.zeros_like(x))

    problems = [(c, p) for c in range(n_chunks) for p in range(n_pairs)]

    def blk(name, cp):
        c, p = cp
        return stage[name][c * ch:(c + 1) * ch, p * LANES:(p + 1) * LANES]

    aakv_s, tcat_s, g2_s, wu_s, ry_s, rt2_s, y0_s, mlr_s, n0_s = handoff

    g1, pw, tinv = {}, {}, {}
    for i, cp in enumerate(problems):
        bt, kt = blk("bt", cp), blk("kt", cp)
        x = jnp.concatenate([blk("at", cp), blk("rt", cp)], axis=0)
        ym = jnp.concatenate([keep(m0, bt), keep(m0, kt), keep(m1, kt), keep(m1, bt)], axis=0)
        g_all = _dot_nt(x, ym)
        g1[cp] = jnp.where(strict_lower, g_all[:ch], 0.0)
        g2_s[i] = jnp.where(incl_lower, g_all[ch:], 0.0).astype(BF16)
        pw[cp] = jnp.where(m0, g1[cp][:, :LANES], g1[cp][:, LANES:])
        tinv[cp] = eye2 + pw[cp]
    yield

    def block_diag_of(cat):
        cat = cat.astype(BF16)
        return jnp.concatenate([keep(m0, cat), keep(m1, cat)], axis=0)

    n_sq = int(math.log2(ch)) - 1
    for cp in problems:
        pw[cp] = _dot(pw[cp].astype(BF16), block_diag_of(pw[cp]))
    yield
    for _ in range(n_sq - 1):
        for cp in problems:
            both = _dot(jnp.concatenate([tinv[cp], pw[cp]], axis=0).astype(BF16), block_diag_of(pw[cp]))
            tinv[cp] = tinv[cp] + both[:ch]
            pw[cp] = both[ch:]
        yield
    for cp in problems:
        tinv[cp] = tinv[cp] + _dot(tinv[cp].astype(BF16), block_diag_of(pw[cp]))
    yield
    for i, cp in enumerate(problems):
        vb = blk("vb", cp)
        v_stack = jnp.concatenate([zeros_cv, keep(m0, vb), keep(m1, vb), zeros_cv], axis=0)
        aakv_s[i] = _dot(g1[cp].astype(BF16), v_stack).astype(BF16)
        tcat_s[i] = tinv[cp].astype(BF16)
    yield
    for i, cp in enumerate(problems):
        j = zero_off + i
        rhs = jnp.concatenate([blk("at", cp), aakv_s[j]], axis=1)
        wu_s[i] = _dot(tcat_s[j], jnp.concatenate([keep(m0w, rhs), keep(m1w, rhs)], axis=0))
    yield
    for i, cp in enumerate(problems):
        j = zero_off + i
        upper = wu_s[j].astype(BF16)
        lower = jnp.concatenate([zeros_cv, blk("vb", cp)], axis=1)
        rhs = jnp.concatenate([keep(m0w, upper), keep(m0w, lower), keep(m1w, lower), keep(m1w, upper)], axis=0)
        ry_s[i] = _dot(g2_s[j], rhs)
    yield
    for i, cp in enumerate(problems):
        j = zero_off + i
        wu, ry = wu_s[j], ry_s[j]
        bkh = jnp.concatenate([blk("bh", cp), blk("kh", cp)], axis=0)
        rt2_s[i] = (blk("rt", cp).astype(F32) + ry[:, :LANES]).astype(BF16)
        y0_s[i] = ry[:, LANES:]
        wt_t = wu[:, :LANES].T.astype(BF16)
        mlr_s[i] = jnp.where(block_diag, _dot(wt_t, bkh[:ch]), 0.0).astype(BF16)
        uv_t = jnp.concatenate([wu[:, LANES:], blk("vb", cp).astype(F32)], axis=0).T.astype(BF16)
        n0_s[i] = jnp.where(block_diag, _dot(uv_t, bkh), 0.0)
    yield

    for c in range(n_chunks):
        rows = slice(c * ch, (c + 1) * ch)
        for p in range(n_pairs):
            cs = slice(p * LANES, (p + 1) * LANES)
            j = zero_off + (c * n_pairs + p)
            ht = state_ref[p]
            htb = ht.astype(BF16)
            y_ref[rows, cs] = _dot_nt(rt2_s[j], htb) + y0_s[j]
            state_ref[p] = ht * stage["egc"][c, 0:1, cs] + _dot(htb, mlr_s[j]) + n0_s[j]
    yield

    seg = seg_ref[...]
    y = y_ref[...]
    inv_hd = 1.0 / hd
    mu = _seg_sum(y, seg) * inv_hd
    yc = y - mu
    var = _seg_sum(yc * yc, seg) * inv_hd
    yn = yc * lax.rsqrt(var + GN_EPS) * gw_ref[...] + gb_ref[...]
    o_ref[...] = ((yn + stage["bonus"][...]) * stage["gate"][...]).astype(o_ref.dtype)


def _interleave(streams, order):
    for i in order:
        next(streams[i], None)
    for g in streams:
        for _ in g:
            pass


_RWKV_TRACE_ORDER = (0, 1, 0, 0, 1, 0, 0, 0, 0, 0, 1, 0, 1, 0, 1, 0, 1, 0, 0)


def _rwkv_kernel(*refs, lora_dims):
    n_in = 15
    (r_ref, k_ref, v_ref, xl_ref, wl_ref, w0_ref, a0_ref, kk_ref, ka_ref, rk_ref, gw_ref, gb_ref, seg_ref,
     tri_ref, zero_ref) = refs[:n_in]
    o_ref = refs[n_in]
    scratch = refs[n_in + 1:]
    state_ref, y_ref = scratch[:2]
    handoff = scratch[2:2 + _N_HANDOFF]
    ns = len(_STAGE_NAMES)
    stages = [dict(zip(_STAGE_NAMES, scratch[2 + _N_HANDOFF + i * ns: 2 + _N_HANDOFF + (i + 1) * ns]))
              for i in range(2)]
    t = pl.program_id(2)

    @pl.when(t == 0)
    def _():
        state_ref[...] = jnp.zeros_like(state_ref)
        for ref in stages[1].values():
            ref[...] = jnp.zeros_like(ref)

    for parity in range(2):
        @pl.when(t % 2 == parity)
        def _(parity=parity):
            _interleave([
                _rwkv_solve(stages[1 - parity], state_ref, y_ref, o_ref, gw_ref, gb_ref, seg_ref, handoff,
                            zero_ref[0]),
                _rwkv_prepare(r_ref, k_ref, v_ref, xl_ref, wl_ref, w0_ref, a0_ref, kk_ref, ka_ref, rk_ref,
                              seg_ref, tri_ref, stages[parity], lora_dims)], _RWKV_TRACE_ORDER)


def _rwkv7(zr, zl, w0, w_decay_up, a0, w_aaa_up, w_gate_up, k_k, k_a, r_k, gn_w, gn_b, batch, seq):
    tok = zr.shape[0]
    width = zr.shape[1] // 3
    lw = zl.shape[1]
    cw = RWKV_HPB * RWKV_HEAD_DIM
    nhb = width // cw
    tblk = RWKV_TBLK
    nt = seq // tblk
    ch = RWKV_CHUNK

    def pad_rows(w, start):
        return jnp.pad(w, ((start, lw - start - w.shape[0]), (0, 0)))

    wl = jnp.stack([pad_rows(w_decay_up, 0), pad_rows(w_aaa_up, DECAY_LORA),
                    pad_rows(w_gate_up, DECAY_LORA + AAA_LORA)], axis=0)
    wl = wl.reshape(3, lw, nhb, cw).transpose(2, 1, 0, 3).reshape(nhb, lw, 3 * cw).astype(BF16)

    idx = jnp.arange(MXU_DIM) // RWKV_HEAD_DIM
    seg = (idx[:, None] == idx[None, :]).astype(BF16)
    tri = (jnp.arange(ch)[:, None] >= jnp.arange(ch)[None, :]).astype(BF16)

    def in_blk(b, t):
        return b * nt + jnp.minimum(t, nt - 1)

    def col(sec):
        return pl.BlockSpec((tblk, cw), lambda b, h, t: (in_blk(b, t), sec * nhb + h))

    vec = pl.BlockSpec((1, cw), lambda b, h, t: (0, h))
    row_vec = lambda a: a.reshape(1, width)
    stage_shapes = ([pltpu.VMEM((tblk, cw), BF16)] * len(_STAGE_BF16)
                    + [pltpu.VMEM((tblk // ch, 8, cw), F32),
                       pltpu.VMEM((tblk, cw), F32), pltpu.VMEM((tblk, cw), F32)])
    n_prob = (tblk // ch) * (cw // LANES)
    handoff_shapes = [
        pltpu.VMEM((n_prob, ch, LANES), BF16),
        pltpu.VMEM((n_prob, ch, LANES), BF16),
        pltpu.VMEM((n_prob, ch, 2 * LANES), BF16),
        pltpu.VMEM((n_prob, ch, 2 * LANES), F32),
        pltpu.VMEM((n_prob, ch, 2 * LANES), F32),
        pltpu.VMEM((n_prob, ch, LANES), BF16), pltpu.VMEM((n_prob, ch, LANES), F32),
        pltpu.VMEM((n_prob, LANES, LANES), BF16), pltpu.VMEM((n_prob, LANES, LANES), F32),
    ]
    assert len(handoff_shapes) == _N_HANDOFF
    kern = functools.partial(_rwkv_kernel, lora_dims=(DECAY_LORA, AAA_LORA, GATE_LORA))
    return pl.pallas_call(
        kern,
        grid=(batch, nhb, nt + 1),
        in_specs=[
            col(0), col(1), col(2),
            pl.BlockSpec((tblk, lw), lambda b, h, t: (in_blk(b, t), 0)),
            pl.BlockSpec((None, lw, 3 * cw), lambda b, h, t: (h, 0, 0)),
            vec, vec, vec, vec, vec, vec, vec,
            pl.BlockSpec((MXU_DIM, MXU_DIM), lambda b, h, t: (0, 0)),
            pl.BlockSpec((ch, ch), lambda b, h, t: (0, 0)),
            pl.BlockSpec(memory_space=pltpu.SMEM),
        ],
        out_specs=pl.BlockSpec((tblk, cw), lambda b, h, t: (b * nt + jnp.maximum(t - 1, 0), h)),
        out_shape=jax.ShapeDtypeStruct((tok, width), BF16),
        scratch_shapes=[
            pltpu.VMEM((cw // LANES, LANES, LANES), F32),
            pltpu.VMEM((tblk, cw), F32),
        ] + handoff_shapes + stage_shapes * 2,
        compiler_params=_cparams(("parallel", "parallel", "arbitrary")),
        name="rwkv7_chunked",
    )(zr, zr, zr, zl, wl,
      row_vec(w0), row_vec(a0), row_vec(k_k), row_vec(k_a), row_vec(r_k), row_vec(gn_w), row_vec(gn_b),
      seg, tri, jnp.zeros((1,), jnp.int32))


def _merge_kernel(attn_ref, rw_ref, g0_ref, g1_ref, wa_ref, wr_ref, out_ref):
    attn_d = _dot(attn_ref[...], wa_ref[...].astype(BF16))
    rwkv_d = _dot(rw_ref[...], wr_ref[...].astype(BF16))
    out_ref[...] = (g0_ref[...].astype(F32) * attn_d + g1_ref[...].astype(F32) * rwkv_d).astype(out_ref.dtype)


def _merge(attn, rw, gates, w_attn_up, w_rwkv_up, tm=1024, tn=1024):
    tok = rw.shape[0]
    d = w_attn_up.shape[1]
    nj = d // tn
    return pl.pallas_call(
        _merge_kernel,
        grid=(tok // tm, nj),
        in_specs=[
            pl.BlockSpec((tm, attn.shape[1]), lambda i, j: (i, 0)),
            pl.BlockSpec((tm, rw.shape[1]), lambda i, j: (i, 0)),
            pl.BlockSpec((tm, tn), lambda i, j: (i, j)),
            pl.BlockSpec((tm, tn), lambda i, j: (i, nj + j)),
            pl.BlockSpec((w_attn_up.shape[0], tn), lambda i, j: (0, j)),
            pl.BlockSpec((w_rwkv_up.shape[0], tn), lambda i, j: (0, j)),
        ],
        out_specs=pl.BlockSpec((tm, tn), lambda i, j: (i, j)),
        out_shape=jax.ShapeDtypeStruct((tok, d), BF16),
        compiler_params=_cparams(("parallel", "arbitrary")),
        name="merge_gated_up",
    )(attn, rw, gates, gates, w_attn_up, w_rwkv_up)


def _ple_kernel(h_ref, wg_ref, p_ref, wp_ref, x_ref, o_ref):
    gate = _sigmoid(_dot(h_ref[...], wg_ref[...].astype(BF16)))
    o_ref[...] = x_ref[...] + gate * _dot(p_ref[...], wp_ref[...].astype(BF16))


def _ple(hn, w_gate, p, w_proj, x, tm=1024, tn=512):
    m, k = hn.shape
    n = w_gate.shape[1]
    kp = p.shape[1]
    return pl.pallas_call(
        _ple_kernel,
        grid=(m // tm, n // tn),
        in_specs=[
            pl.BlockSpec((tm, k), lambda i, j: (i, 0)),
            pl.BlockSpec((k, tn), lambda i, j: (0, j)),
            pl.BlockSpec((tm, kp), lambda i, j: (i, 0)),
            pl.BlockSpec((kp, tn), lambda i, j: (0, j)),
            pl.BlockSpec((tm, tn), lambda i, j: (i, j)),
        ],
        out_specs=pl.BlockSpec((tm, tn), lambda i, j: (i, j)),
        out_shape=jax.ShapeDtypeStruct((m, n), F32),
        compiler_params=_cparams(("parallel", "arbitrary")),
        name="ple_gated",
    )(hn, w_gate, p, w_proj, x)


def _layer(x2d, p2d, batch, seq, norm_mix, w_in, q_gain, k_gain, rel_bias, w_attn_up, shift_mix, w0,
           w_decay_up, a0, w_aaa_up, w_gate_up, k_k, k_a, r_k, gn_w, gn_b, w_rwkv_up, w_out, norm_mlp,
           w_mlp_in, w_mlp_out, norm_ple, w_ple_gate, w_ple_proj):
    n_attn_heads = len(DILATED_GROUPS) * ATTN_HEADS_PER_GROUP
    attn_width = n_attn_heads * ATTN_HEAD_DIM
    rwkv_width = w_rwkv_up.shape[0]
    a_end = 3 * attn_width
    z_end = a_end + 3 * rwkv_width
    r_end = z_end + DECAY_LORA + AAA_LORA + GATE_LORA

    h = _rmsnorm_bf16(x2d, norm_mix)

    gains = jnp.stack([q_gain * (ATTN_HEAD_DIM ** -0.5), k_gain, jnp.ones_like(q_gain)]).reshape(3, 1, LANES)
    w_in_t = jnp.swapaxes(w_in, 0, 1)
    tn = ATTN_HEADS_PER_GROUP * ATTN_HEAD_DIM
    qkv_groups = [_qkv_proj(h, w_in_t, gains, gi, dilation, batch, seq)
                  for gi, (_, dilation) in enumerate(DILATED_GROUPS)]
    mix_z = shift_mix[: z_end - a_end].reshape(1, -1)
    zr = _matmul_token_shift(h, w_in_t, mix_z, seq, n=z_end - a_end, wt_row_blk=a_end // tn, tn=tn,
                             name="w_in_rkv")
    mix_l = jnp.pad(shift_mix[z_end - a_end:], (0, tn - (r_end - z_end))).reshape(1, tn)
    zl = _matmul_token_shift(h, w_in_t, mix_l, seq, n=tn, wt_row_blk=z_end // tn, tn=tn, name="w_in_lora")
    gates = _matmul(h, w_in_t, n=w_in.shape[1] - r_end, wt_row0=r_end, act="sigmoid", out_dtype=BF16,
                    name="w_in_gates")

    attn = _dilated_attention(qkv_groups, rel_bias, batch, seq)

    rw = _rwkv7(zr, zl, w0, w_decay_up, a0, w_aaa_up, w_gate_up, k_k, k_a, r_k, gn_w, gn_b, batch, seq)

    merged = _merge(attn, rw, gates, w_attn_up, w_rwkv_up)
    x1 = _matmul_residual(merged, w_out, x2d, name="w_out_res")

    h2 = _rmsnorm_bf16(x1, norm_mlp)
    u = _matmul(h2, w_mlp_in, act="relu2", out_dtype=BF16, name="mlp_in")
    x2 = _matmul_kacc_residual(u, w_mlp_out, x1, name="mlp_out_res")

    hn = _rmsnorm_bf16(x2, norm_ple)
    return _ple(hn, w_ple_gate, p2d.astype(BF16), w_ple_proj, x2)


def kernel(x, p, norm_mix, w_in, q_gain, k_gain, rel_bias, w_attn_up, shift_mix, w0, w_decay_up, a0, w_aaa_up,
           w_gate_up, k_k, k_a, r_k, gn_w, gn_b, w_rwkv_up, w_out, norm_mlp, w_mlp_in, w_mlp_out, norm_ple,
           w_ple_gate, w_ple_proj):
    batch, seq, d_model = x.shape
    depth = p.shape[0]
    x2d = x.reshape(batch * seq, d_model)
    for i in range(depth):
        x2d = _layer(x2d, p[i].reshape(batch * seq, -1), batch, seq, norm_mix[i], w_in[i], q_gain[i], k_gain[i],
                     rel_bias, w_attn_up[i], shift_mix[i], w0[i], w_decay_up[i], a0[i], w_aaa_up[i],
                     w_gate_up[i], k_k[i], k_a[i], r_k[i], gn_w[i], gn_b[i], w_rwkv_up[i], w_out[i],
                     norm_mlp[i], w_mlp_in[i], w_mlp_out[i], norm_ple[i], w_ple_gate[i], w_ple_proj[i])
    return x2d.reshape(batch, seq, d_model)
```
